```python
import math
import jax, jax.numpy as jnp
from jax import lax
import numpy as np

D_MODEL = 4096
BATCH = 2
SEQ = 4096
DEPTH = 1
DEC_BATCH = 32
DEC_SEQ = 1
PAST_LEN = 8192
PAGE_SIZE = 128

ATT_HEADS = D_MODEL // 512
QK_DIM = 128
V_DIM = 2 * QK_DIM
ATT_WIDTH = ATT_HEADS * V_DIM
QK_COLS = ATT_HEADS * 2 * QK_DIM
SSM_WIDTH = D_MODEL // 2
SSM_GROUP = 16
SSM_GROUPS = SSM_WIDTH // SSM_GROUP
SSM_STATE = 64
D_FF = ((8 * D_MODEL) // 3 + 255) // 256 * 256
Q_BLOCK = 128
N_SUB = 3
EPS = 1e-6
NEG = -1e30
SPLITS = (QK_COLS, 2 * QK_COLS, 2 * QK_COLS + ATT_WIDTH,
          2 * QK_COLS + ATT_WIDTH + SSM_WIDTH,
          2 * QK_COLS + ATT_WIDTH + SSM_WIDTH + D_MODEL)
IN_COLS = 2 * QK_COLS + ATT_WIDTH + SSM_WIDTH + 2 * D_MODEL

kernel_name = "hybrid_s5_diffattn_macaron_step"

F32 = jnp.float32


def rmsnorm(x, g):
    xf = x.astype(F32)
    y = xf * lax.rsqrt(jnp.mean(xf * xf, axis=-1, keepdims=True) + EPS)
    return (y * g.astype(F32)).astype(x.dtype)


def modulate(x, g, shift, scale):
    return rmsnorm(x, g) * (1 + scale[:, None, :]) + shift[:, None, :]


def swiglu(h, w_gate, w_up, w_down):
    return (jax.nn.silu(h @ w_gate) * (h @ w_up)) @ w_down


def diff_lambda(lam_q, lam_k, lam_init):
    lq, lk = lam_q.astype(F32), lam_k.astype(F32)
    return jnp.exp(jnp.sum(lq[0] * lk[0])) - jnp.exp(jnp.sum(lq[1] * lk[1])) + lam_init


def diff_attn_prompt(q, k, v, lam):
    bsz, s_len = q.shape[:2]
    nblk = s_len // Q_BLOCK
    scale = QK_DIM ** -0.5
    kf, vf = k.astype(F32), v.astype(F32)
    qb = q.astype(F32).reshape(bsz, nblk, Q_BLOCK, ATT_HEADS, 2, QK_DIM).swapaxes(0, 1)
    kpos = jnp.arange(s_len)

    def block(args):
        qi, i = args
        s = jnp.einsum('bqhmd,bkhmd->bhmqk', qi, kf) * scale
        qpos = i * Q_BLOCK + jnp.arange(Q_BLOCK)
        mask = kpos[None, :] <= qpos[:, None]
        p = jax.nn.softmax(jnp.where(mask, s, NEG), axis=-1)
        w = p[:, :, 0] - lam * p[:, :, 1]
        return jnp.einsum('bhqk,bkhv->bqhv', w, vf)

    o = lax.map(block, (qb, jnp.arange(nblk)))
    return o.swapaxes(0, 1).reshape(bsz, s_len, ATT_HEADS, V_DIM).astype(v.dtype)


def diff_attn_sample(q, k_new, v_new, lam, cache_k, cache_v, page_table):
    bd, t_len = q.shape[:2]
    qf = q.astype(F32) * QK_DIM ** -0.5

    def update(carry, k, v, mask):
        m, l, acc = carry
        s = jnp.einsum('bthmd,bphmd->bhmtp', qf, k.astype(F32))
        if mask is not None:
            s = jnp.where(mask, s, NEG)
        m_new = jnp.maximum(m, jnp.max(s, axis=-1))
        corr = jnp.exp(m - m_new)
        p = jnp.exp(s - m_new[..., None])
        l = l * corr + jnp.sum(p, axis=-1)
        acc = acc * corr[..., None] + jnp.einsum('bhmtp,bphv->bhmtv', p, v.astype(F32))
        return m_new, l, acc

    init = (jnp.full((bd, ATT_HEADS, 2, t_len), NEG, F32),
            jnp.zeros((bd, ATT_HEADS, 2, t_len), F32),
            jnp.zeros((bd, ATT_HEADS, 2, t_len, V_DIM), F32))

    def page_step(carry, pages):
        k = cache_k[pages].reshape(bd, PAGE_SIZE, ATT_HEADS, 2, QK_DIM)
        v = cache_v[pages]
        return update(carry, k, v, None), None

    carry, _ = lax.scan(page_step, init, page_table.T)
    tpos = jnp.arange(t_len)
    mask = tpos[None, :] <= tpos[:, None]
    _, l, acc = update(carry, k_new, v_new, mask)
    w = acc / l[..., None]
    o = w[:, :, 0] - lam * w[:, :, 1]
    return o.transpose(0, 2, 1, 3).astype(v_new.dtype)


def s5_branch(u, x0_re, x0_im, lam_re, lam_im, log_dt, b_re, b_im, c_re, c_im, d, w_glu):
    bsz, L = u.shape[:2]
    ug = u.astype(F32).reshape(bsz, L, SSM_GROUPS, SSM_GROUP)
    dt = jnp.exp(log_dt.astype(F32))[:, None]
    lr, li = lam_re.astype(F32), lam_im.astype(F32)
    mag = jnp.exp(dt * lr)
    a_re, a_im = mag * jnp.cos(dt * li), mag * jnp.sin(dt * li)
    den = lr * lr + li * li
    f_re = ((a_re - 1) * lr + a_im * li) / den
    f_im = (a_im * lr - (a_re - 1) * li) / den
    br, bi = b_re.astype(F32), b_im.astype(F32)
    bb_re = f_re[..., None] * br - f_im[..., None] * bi
    bb_im = f_re[..., None] * bi + f_im[..., None] * br
    bu_re = jnp.einsum('gns,blgs->blgn', bb_re, ug)
    bu_im = jnp.einsum('gns,blgs->blgn', bb_im, ug)
    xr0, xi0 = x0_re.astype(F32), x0_im.astype(F32)
    bu_re = bu_re.at[:, 0].add(a_re * xr0 - a_im * xi0)
    bu_im = bu_im.at[:, 0].add(a_re * xi0 + a_im * xr0)
    ar = jnp.broadcast_to(a_re, bu_re.shape)
    ai = jnp.broadcast_to(a_im, bu_im.shape)

    def combine(e1, e2):
        a1r, a1i, b1r, b1i = e1
        a2r, a2i, b2r, b2i = e2
        return (a2r * a1r - a2i * a1i, a2r * a1i + a2i * a1r,
                a2r * b1r - a2i * b1i + b2r, a2r * b1i + a2i * b1r + b2i)

    _, _, xr, xi = lax.associative_scan(combine, (ar, ai, bu_re, bu_im), axis=1)
    y = (jnp.einsum('gsn,blgn->blgs', c_re.astype(F32), xr)
         - jnp.einsum('gsn,blgn->blgs', c_im.astype(F32), xi)
         + d.astype(F32).reshape(SSM_GROUPS, SSM_GROUP) * ug)
    y = jax.nn.gelu(y.reshape(bsz, L, SSM_WIDTH))
    y = y * jax.nn.sigmoid(y @ w_glu)
    return y.astype(u.dtype), xr[:, -1], xi[:, -1]


def run_layer(x, c, lp, attend, x0_re, x0_im, lam_init):
    bsz, L = x.shape[:2]
    mod = (jax.nn.silu(c.astype(F32)) @ lp['w_mod'] + lp['b_mod'])
    mod = mod.reshape(bsz, N_SUB, 3, D_MODEL).astype(x.dtype)
    g_pre, g_post = lp['norm_pre'], lp['norm_post']
    h = modulate(x, g_pre[0], mod[:, 0, 0], mod[:, 0, 1])
    y = swiglu(h, lp['ffn1_gate'], lp['ffn1_up'], lp['ffn1_down'])
    x = x + 0.5 * mod[:, 0, 2][:, None, :] * rmsnorm(y, g_post[0])
    h = modulate(x, g_pre[1], mod[:, 1, 0], mod[:, 1, 1])
    z = h @ lp['w_in']
    q, k, v, u, g_att, g_ssm = jnp.split(z, SPLITS, axis=-1)
    q = q.reshape(bsz, L, ATT_HEADS, 2, QK_DIM)
    k = k.reshape(bsz, L, ATT_HEADS, 2, QK_DIM)
    v = v.reshape(bsz, L, ATT_HEADS, V_DIM)
    lam = diff_lambda(lp['lam_q'], lp['lam_k'], lam_init)
    o = attend(q, k, v, lam)
    o = rmsnorm(o, lp['attn_subln']) * (1 - lam_init)
    y_att = o.reshape(bsz, L, ATT_WIDTH) @ lp['w_branch_attn']
    y_s, s_re, s_im = s5_branch(u, x0_re, x0_im, lp['ssm_lam_re'], lp['ssm_lam_im'], lp['ssm_log_dt'],
                                lp['ssm_b_re'], lp['ssm_b_im'], lp['ssm_c_re'], lp['ssm_c_im'],
                                lp['ssm_d'], lp['ssm_w_glu'])
    y_ssm = y_s @ lp['w_branch_ssm']
    merged = jax.nn.sigmoid(g_att) * y_att + jax.nn.sigmoid(g_ssm) * y_ssm
    y = merged @ lp['w_out']
    x = x + mod[:, 1, 2][:, None, :] * rmsnorm(y, g_post[1])
    h = modulate(x, g_pre[2], mod[:, 2, 0], mod[:, 2, 1])
    y = swiglu(h, lp['ffn2_gate'], lp['ffn2_up'], lp['ffn2_down'])
    x = x + 0.5 * mod[:, 2, 2][:, None, :] * rmsnorm(y, g_post[2])
    k_rows = k.reshape(bsz, L, ATT_HEADS, 2 * QK_DIM)
    return x, k_rows, v, s_re, s_im


def setup_inputs(seed: int = 0) -> dict:
    key = jax.random.key(seed)
    ks = iter(jax.random.split(key, 48))
    n_pages = PAST_LEN // PAGE_SIZE
    n_pool = (DEC_BATCH * n_pages * 5) // 4

    def nrm(shape, scale):
        return jax.random.normal(next(ks), shape, F32) * scale

    def gain(shape):
        return 1.0 + 0.05 * jax.random.normal(next(ks), shape, F32)

    inp = {}
    inp['x_prompt'] = nrm((BATCH, SEQ, D_MODEL), 1.0)
    inp['x_sample'] = nrm((DEC_BATCH, DEC_SEQ, D_MODEL), 1.0)
    inp['cache_k'] = nrm((DEPTH, n_pool, PAGE_SIZE, ATT_HEADS, 2 * QK_DIM), 1.0)
    inp['cache_v'] = nrm((DEPTH, n_pool, PAGE_SIZE, ATT_HEADS, V_DIM), 1.0)
    inp['state_ssm_re'] = nrm((DEPTH, DEC_BATCH, SSM_GROUPS, SSM_STATE), 0.1)
    inp['state_ssm_im'] = nrm((DEPTH, DEC_BATCH, SSM_GROUPS, SSM_STATE), 0.1)
    perm = jax.random.permutation(next(ks), n_pool)[:DEC_BATCH * n_pages]
    inp['page_table'] = perm.reshape(DEC_BATCH, n_pages).astype(jnp.int32)
    inp['c_prompt'] = nrm((BATCH, D_MODEL), 1.0)
    inp['c_sample'] = nrm((DEC_BATCH, D_MODEL), 1.0)
    dm = D_MODEL ** -0.5
    inp['w_mod'] = nrm((DEPTH, D_MODEL, N_SUB * 3 * D_MODEL), 0.5 * dm)
    inp['b_mod'] = nrm((DEPTH, N_SUB * 3 * D_MODEL), 0.01)
    inp['norm_pre'] = gain((DEPTH, N_SUB, D_MODEL))
    inp['norm_post'] = gain((DEPTH, N_SUB, D_MODEL))
    inp['ffn1_gate'] = nrm((DEPTH, D_MODEL, D_FF), dm)
    inp['ffn1_up'] = nrm((DEPTH, D_MODEL, D_FF), dm)
    inp['ffn1_down'] = nrm((DEPTH, D_FF, D_MODEL), D_FF ** -0.5)
    inp['w_in'] = nrm((DEPTH, D_MODEL, IN_COLS), dm)
    inp['lam_q'] = nrm((DEPTH, 2, QK_DIM), 0.1)
    inp['lam_k'] = nrm((DEPTH, 2, QK_DIM), 0.1)
    inp['attn_subln'] = gain((DEPTH, V_DIM))
    inp['w_branch_attn'] = nrm((DEPTH, ATT_WIDTH, D_MODEL), ATT_WIDTH ** -0.5)
    inp['ssm_lam_re'] = -0.5 + nrm((DEPTH, SSM_GROUPS, SSM_STATE), 0.01)
    inp['ssm_lam_im'] = (math.pi * jnp.arange(SSM_STATE, dtype=F32))[None, None, :] + nrm((DEPTH, SSM_GROUPS, SSM_STATE), 0.01)
    inp['ssm_log_dt'] = jax.random.uniform(next(ks), (DEPTH, SSM_GROUPS), F32, math.log(0.001), math.log(0.1))
    inp['ssm_b_re'] = nrm((DEPTH, SSM_GROUPS, SSM_STATE, SSM_GROUP), (2 * SSM_GROUP) ** -0.5)
    inp['ssm_b_im'] = nrm((DEPTH, SSM_GROUPS, SSM_STATE, SSM_GROUP), (2 * SSM_GROUP) ** -0.5)
    inp['ssm_c_re'] = nrm((DEPTH, SSM_GROUPS, SSM_GROUP, SSM_STATE), (2 * SSM_STATE) ** -0.5)
    inp['ssm_c_im'] = nrm((DEPTH, SSM_GROUPS, SSM_GROUP, SSM_STATE), (2 * SSM_STATE) ** -0.5)
    inp['ssm_d'] = nrm((DEPTH, SSM_WIDTH), 1.0)
    inp['ssm_w_glu'] = nrm((DEPTH, SSM_WIDTH, SSM_WIDTH), SSM_WIDTH ** -0.5)
    inp['w_branch_ssm'] = nrm((DEPTH, SSM_WIDTH, D_MODEL), SSM_WIDTH ** -0.5)
    inp['w_out'] = nrm((DEPTH, D_MODEL, D_MODEL), dm)
    inp['ffn2_gate'] = nrm((DEPTH, D_MODEL, D_FF), dm)
    inp['ffn2_up'] = nrm((DEPTH, D_MODEL, D_FF), dm)
    inp['ffn2_down'] = nrm((DEPTH, D_FF, D_MODEL), D_FF ** -0.5)
    return inp


def reference(x_prompt, x_sample, cache_k, cache_v, state_ssm_re, state_ssm_im, page_table,
              c_prompt, c_sample, w_mod, b_mod, norm_pre, norm_post, ffn1_gate, ffn1_up, ffn1_down,
              w_in, lam_q, lam_k, attn_subln, w_branch_attn, ssm_lam_re, ssm_lam_im, ssm_log_dt,
              ssm_b_re, ssm_b_im, ssm_c_re, ssm_c_im, ssm_d, ssm_w_glu, w_branch_ssm, w_out,
              ffn2_gate, ffn2_up, ffn2_down):
    xp, xs = x_prompt, x_sample
    kp, vp, srp, sip, kss, vss, srs, sis = [], [], [], [], [], [], [], []
    for li in range(DEPTH):
        lp = dict(w_mod=w_mod[li], b_mod=b_mod[li], norm_pre=norm_pre[li], norm_post=norm_post[li],
                  ffn1_gate=ffn1_gate[li], ffn1_up=ffn1_up[li], ffn1_down=ffn1_down[li],
                  w_in=w_in[li], lam_q=lam_q[li], lam_k=lam_k[li], attn_subln=attn_subln[li],
                  w_branch_attn=w_branch_attn[li], ssm_lam_re=ssm_lam_re[li], ssm_lam_im=ssm_lam_im[li],
                  ssm_log_dt=ssm_log_dt[li], ssm_b_re=ssm_b_re[li], ssm_b_im=ssm_b_im[li],
                  ssm_c_re=ssm_c_re[li], ssm_c_im=ssm_c_im[li], ssm_d=ssm_d[li], ssm_w_glu=ssm_w_glu[li],
                  w_branch_ssm=w_branch_ssm[li], w_out=w_out[li],
                  ffn2_gate=ffn2_gate[li], ffn2_up=ffn2_up[li], ffn2_down=ffn2_down[li])
        lam_init = 0.8 - 0.6 * math.exp(-0.3 * li)
        ck, cv = cache_k[li], cache_v[li]

        def attend_sample(q, k, v, lam, ck=ck, cv=cv):
            return diff_attn_sample(q, k, v, lam, ck, cv, page_table)

        zeros_state = jnp.zeros((xp.shape[0], SSM_GROUPS, SSM_STATE), F32)
        xp, k1, v1, r1, i1 = run_layer(xp, c_prompt, lp, diff_attn_prompt, zeros_state, zeros_state, lam_init)
        xs, k2, v2, r2, i2 = run_layer(xs, c_sample, lp, attend_sample, state_ssm_re[li], state_ssm_im[li], lam_init)
        kp.append(k1); vp.append(v1); srp.append(r1); sip.append(i1)
        kss.append(k2); vss.append(v2); srs.append(r2); sis.append(i2)
    return (xp, xs, jnp.stack(kp), jnp.stack(vp), jnp.stack(srp), jnp.stack(sip),
            jnp.stack(kss), jnp.stack(vss), jnp.stack(srs), jnp.stack(sis))
```

```python
import functools
import math

import numpy as np
import jax
import jax.numpy as jnp
from jax import lax
from jax.experimental import pallas as pl
from jax.experimental.pallas import tpu as pltpu

F32 = jnp.float32
BF16 = jnp.bfloat16
EPS = 1e-6
NEG = -1e30

QK_DIM = 128
HEAD_COLS = 2 * QK_DIM
SSM_GROUP = 16
SSM_STATE = 64
N_SUB = 3
SUBLANES = 8
LANES = 128
SSM_SUPER = 256
SSM_SEG = 64
VMEM_LIMIT_BYTES = 56 * 1024 * 1024


def _params(*sem):
    return pltpu.CompilerParams(dimension_semantics=sem, vmem_limit_bytes=VMEM_LIMIT_BYTES)


def _round_up(x, m):
    return (x + m - 1) // m * m


def _sigmoid(x):
    return 1.0 / (1.0 + jnp.exp(-x))


def _rms(x, gain):
    return x * lax.rsqrt(jnp.mean(x * x, axis=-1, keepdims=True) + EPS) * gain


def _mm_plain_kernel(x_ref, w_ref, *o_refs, scale):
    acc = jnp.dot(x_ref[...], w_ref[...], preferred_element_type=F32)
    if scale != 1.0:
        acc = acc * scale
    for o_ref in o_refs:
        o_ref[...] = acc.astype(o_ref.dtype)


def matmul_cols(x, w, col0, n_cols, out_dtypes, *, tm=1024, tn=512, scale=1.0):
    m, k = x.shape
    tm, tn = min(tm, m), min(tn, n_cols)
    off = col0 // tn
    return pl.pallas_call(
        functools.partial(_mm_plain_kernel, scale=scale),
        grid=(m // tm, n_cols // tn),
        in_specs=[pl.BlockSpec((tm, k), lambda i, j: (i, 0)),
                  pl.BlockSpec((k, tn), lambda i, j: (0, j + off))],
        out_specs=[pl.BlockSpec((tm, tn), lambda i, j: (i, j)) for _ in out_dtypes],
        out_shape=[jax.ShapeDtypeStruct((m, n_cols), dt) for dt in out_dtypes],
        compiler_params=_params("parallel", "arbitrary"),
    )(x, w)


def _mm_kgrid_kernel(x_ref, w_ref, o_ref):
    part = jnp.dot(x_ref[...], w_ref[...], preferred_element_type=F32)

    @pl.when(pl.program_id(2) == 0)
    def _():
        o_ref[...] = part

    @pl.when(pl.program_id(2) != 0)
    def _():
        o_ref[...] += part


def matmul_kgrid(x, w, *, tm=1024, tn=1024, tk):
    m, k = x.shape
    n = w.shape[1]
    tm, tn = min(tm, m), min(tn, n)
    return pl.pallas_call(
        _mm_kgrid_kernel,
        grid=(m // tm, n // tn, k // tk),
        in_specs=[pl.BlockSpec((tm, tk), lambda i, j, kk: (i, kk)),
                  pl.BlockSpec((tk, tn), lambda i, j, kk: (kk, j))],
        out_specs=pl.BlockSpec((tm, tn), lambda i, j, kk: (i, j)),
        out_shape=jax.ShapeDtypeStruct((m, n), F32),
        compiler_params=_params("parallel", "parallel", "arbitrary"),
    )(x, w)


def _mm_swiglu_kernel(x_ref, wg_ref, wu_ref, o_ref):
    x = x_ref[...]
    g = jnp.dot(x, wg_ref[...], preferred_element_type=F32)
    u = jnp.dot(x, wu_ref[...], preferred_element_type=F32)
    o_ref[...] = (g * _sigmoid(g) * u).astype(o_ref.dtype)


def matmul_swiglu(x, wg, wu, *, tm=1024, tn=512):
    m, k = x.shape
    n = wg.shape[1]
    tm, tn = min(tm, m), min(tn, n)
    return pl.pallas_call(
        _mm_swiglu_kernel,
        grid=(m // tm, n // tn),
        in_specs=[pl.BlockSpec((tm, k), lambda i, j: (i, 0)),
                  pl.BlockSpec((k, tn), lambda i, j: (0, j)),
                  pl.BlockSpec((k, tn), lambda i, j: (0, j))],
        out_specs=pl.BlockSpec((tm, tn), lambda i, j: (i, j)),
        out_shape=jax.ShapeDtypeStruct((m, n), BF16),
        compiler_params=_params("parallel", "arbitrary"),
    )(x, wg, wu)


def _mm_glu_kernel(x_ref, w_ref, e_ref, o_ref):
    acc = jnp.dot(x_ref[...].astype(BF16), w_ref[...], preferred_element_type=F32)
    o_ref[...] = (e_ref[...] * _sigmoid(acc)).astype(o_ref.dtype)


def matmul_glu(y, w, *, tm=1024, tn=512):
    m, k = y.shape
    tm, tn = min(tm, m), min(tn, k)
    return pl.pallas_call(
        _mm_glu_kernel,
        grid=(m // tm, k // tn),
        in_specs=[pl.BlockSpec((tm, k), lambda i, j: (i, 0)),
                  pl.BlockSpec((k, tn), lambda i, j: (0, j)),
                  pl.BlockSpec((tm, tn), lambda i, j: (i, j))],
        out_specs=pl.BlockSpec((tm, tn), lambda i, j: (i, j)),
        out_shape=jax.ShapeDtypeStruct((m, k), BF16),
        compiler_params=_params("parallel", "arbitrary"),
    )(y, w, y)


def _mm_merge_kernel(xa_ref, wa_ref, ga_ref, xs_ref, ws_ref, gs_ref, o_ref):
    ya = jnp.dot(xa_ref[...], wa_ref[...], preferred_element_type=F32)
    ys = jnp.dot(xs_ref[...], ws_ref[...], preferred_element_type=F32)
    merged = _sigmoid(ga_ref[...].astype(F32)) * ya + _sigmoid(gs_ref[...].astype(F32)) * ys
    o_ref[...] = merged.astype(o_ref.dtype)


def matmul_merge(xa, wa, ga, xs, ws, gs, *, tm=1024, tn=512):
    m, ka = xa.shape
    ks = xs.shape[1]
    n = wa.shape[1]
    tm, tn = min(tm, m), min(tn, n)
    tile = pl.BlockSpec((tm, tn), lambda i, j: (i, j))
    return pl.pallas_call(
        _mm_merge_kernel,
        grid=(m // tm, n // tn),
        in_specs=[pl.BlockSpec((tm, ka), lambda i, j: (i, 0)),
                  pl.BlockSpec((ka, tn), lambda i, j: (0, j)), tile,
                  pl.BlockSpec((tm, ks), lambda i, j: (i, 0)),
                  pl.BlockSpec((ks, tn), lambda i, j: (0, j)), tile],
        out_specs=tile,
        out_shape=jax.ShapeDtypeStruct((m, n), BF16),
        compiler_params=_params("parallel", "arbitrary"),
    )(xa, wa, ga, xs, ws, gs)


def _mm_mod_kernel(c_ref, w_ref, b_ref, o_ref):
    c = c_ref[...]
    lhs = (c * _sigmoid(c)).astype(BF16)
    acc = jnp.dot(lhs, w_ref[...].astype(BF16), preferred_element_type=F32)
    o_ref[...] = acc + b_ref[...]


def matmul_mod(c, w, b, *, tn=512):
    m, k = c.shape
    n = w.shape[1]
    return pl.pallas_call(
        _mm_mod_kernel,
        grid=(n // tn,),
        in_specs=[pl.BlockSpec((m, k), lambda j: (0, 0)),
                  pl.BlockSpec((k, tn), lambda j: (0, j)),
                  pl.BlockSpec((1, tn), lambda j: (0, j))],
        out_specs=pl.BlockSpec((m, tn), lambda j: (0, j)),
        out_shape=jax.ShapeDtypeStruct((m, n), F32),
        compiler_params=_params("arbitrary"),
    )(c, w, b)


def _modulate_kernel(x_ref, g_ref, sh_ref, sc_ref, h_ref):
    h = _rms(x_ref[0], g_ref[...]) * (1.0 + sc_ref[0]) + sh_ref[0]
    h_ref[0] = h.astype(h_ref.dtype)


def _resid_kernel(x_ref, y_ref, gpost_ref, gate_ref, *rest, coef, with_next):
    xn = x_ref[0] + coef * gate_ref[0] * _rms(y_ref[0], gpost_ref[...])
    if with_next:
        gpre_ref, sh_ref, sc_ref, xo_ref, h_ref = rest
        h = _rms(xn, gpre_ref[...]) * (1.0 + sc_ref[0]) + sh_ref[0]
        h_ref[0] = h.astype(h_ref.dtype)
    else:
        (xo_ref,) = rest
    xo_ref[0] = xn


def _row_specs(x, mod_rows, tr):
    g, t, d = x.shape
    tr = min(tr, t)
    row = pl.BlockSpec((1, tr, d), lambda gi, ti: (gi, ti, 0))
    gain = pl.BlockSpec((1, d), lambda gi, ti: (0, 0))
    if mod_rows == 1:
        mod = pl.BlockSpec((1, 1, d), lambda gi, ti: (gi, 0, 0))
    else:
        mod = pl.BlockSpec((1, tr, d), lambda gi, ti: (gi, ti, 0))
    return (g, t // tr), row, gain, mod


def modulate(x, gain, shift, scale, *, tr=256):
    grid, row, gspec, mod = _row_specs(x, shift.shape[1], tr)
    return pl.pallas_call(
        _modulate_kernel, grid=grid,
        in_specs=[row, gspec, mod, mod], out_specs=row,
        out_shape=jax.ShapeDtypeStruct(x.shape, BF16),
        compiler_params=_params("parallel", "parallel"),
    )(x, gain, shift, scale)


def resid_update(x, y, gpost, gate, coef, nxt=None, *, tr=256):
    grid, row, gspec, mod = _row_specs(x, gate.shape[1], tr)
    in_specs = [row, row, gspec, mod]
    args = [x, y, gpost, gate]
    out_specs = [row]
    out_shape = [jax.ShapeDtypeStruct(x.shape, F32)]
    if nxt is not None:
        in_specs += [gspec, mod, mod]
        args += list(nxt)
        out_specs.append(row)
        out_shape.append(jax.ShapeDtypeStruct(x.shape, BF16))
    outs = pl.pallas_call(
        functools.partial(_resid_kernel, coef=coef, with_next=nxt is not None), grid=grid,
        in_specs=in_specs, out_specs=out_specs, out_shape=out_shape,
        compiler_params=_params("parallel", "parallel"),
    )(*args)
    return outs if nxt is not None else (outs[0], None)


def _diff_lambda(lamq_ref, lamk_ref, lam_init):
    prod = lamq_ref[...] * lamk_ref[...]
    s0 = jnp.sum(prod[0:1], axis=-1, keepdims=True)
    s1 = jnp.sum(prod[1:2], axis=-1, keepdims=True)
    return jnp.exp(s0) - jnp.exp(s1) + lam_init


def _attn_prompt_kernel(qi_tab, ki_tab, lamq_ref, lamk_ref, subln_ref, q_ref, k_ref, v_ref, o_ref,
                        m_sc, l_sc, acc_sc, *, lam_init):
    t = pl.program_id(2)
    qi, ki = qi_tab[t], ki_tab[t]
    tq, tk = q_ref.shape[0], k_ref.shape[0]

    @pl.when(ki == 0)
    def _():
        m_sc[...] = jnp.full(m_sc.shape, NEG, F32)
        l_sc[...] = jnp.zeros(l_sc.shape, F32)
        acc_sc[...] = jnp.zeros(acc_sc.shape, F32)

    def step(diagonal):
        q, k, v = q_ref[...], k_ref[...], v_ref[...]
        for mp in range(2):
            cols = slice(mp * QK_DIM, (mp + 1) * QK_DIM)
            s = lax.dot_general(q[:, cols], k[:, cols], (((1,), (1,)), ((), ())),
                                preferred_element_type=F32)
            if diagonal:
                row = lax.broadcasted_iota(jnp.int32, (tq, tk), 0)
                col = lax.broadcasted_iota(jnp.int32, (tq, tk), 1)
                s = jnp.where(col <= row, s, NEG)
            m_prev = m_sc[mp]
            m_new = jnp.maximum(m_prev, jnp.max(s, axis=-1, keepdims=True))
            corr = jnp.exp(m_prev - m_new)
            p = jnp.exp(s - m_new)
            l_sc[mp] = l_sc[mp] * corr + jnp.sum(p, axis=-1, keepdims=True)
            acc_sc[mp] = acc_sc[mp] * corr + jnp.dot(p.astype(BF16), v, preferred_element_type=F32)
            m_sc[mp] = m_new

    @pl.when(ki < qi)
    def _():
        step(False)

    @pl.when(ki == qi)
    def _():
        step(True)
        lam = _diff_lambda(lamq_ref, lamk_ref, lam_init)
        o = acc_sc[0] / l_sc[0] - lam * (acc_sc[1] / l_sc[1])
        o_ref[...] = (_rms(o, subln_ref[...]) * (1.0 - lam_init)).astype(o_ref.dtype)


def attn_prompt(q, k, v, lam_q, lam_k, subln, bsz, lam_init, *, tq=512):
    m, width = q.shape
    s_len = m // bsz
    heads = width // HEAD_COLS
    tq = min(tq, s_len)
    nq = s_len // tq
    pairs = [(a, b) for a in range(nq) for b in range(a + 1)]
    qi_tab = jnp.asarray(np.array([p[0] for p in pairs], np.int32))
    ki_tab = jnp.asarray(np.array([p[1] for p in pairs], np.int32))
    small = lambda shape: pl.BlockSpec(shape, lambda b, h, t, qt, kt: (0, 0))
    grid_spec = pltpu.PrefetchScalarGridSpec(
        num_scalar_prefetch=2,
        grid=(bsz, heads, len(pairs)),
        in_specs=[small(lam_q.shape), small(lam_k.shape), small(subln.shape),
                  pl.BlockSpec((tq, HEAD_COLS), lambda b, h, t, qt, kt: (b * nq + qt[t], h)),
                  pl.BlockSpec((tq, HEAD_COLS), lambda b, h, t, qt, kt: (b * nq + kt[t], h)),
                  pl.BlockSpec((tq, HEAD_COLS), lambda b, h, t, qt, kt: (b * nq + kt[t], h))],
        out_specs=pl.BlockSpec((tq, HEAD_COLS), lambda b, h, t, qt, kt: (b * nq + qt[t], h)),
        scratch_shapes=[pltpu.VMEM((2, tq, 1), F32), pltpu.VMEM((2, tq, 1), F32),
                        pltpu.VMEM((2, tq, HEAD_COLS), F32)])
    return pl.pallas_call(
        functools.partial(_attn_prompt_kernel, lam_init=lam_init),
        grid_spec=grid_spec,
        out_shape=jax.ShapeDtypeStruct((m, width), BF16),
        compiler_params=_params("parallel", "parallel", "arbitrary"),
    )(qi_tab, ki_tab, lam_q, lam_k, subln, q, k, v)


def _attn_decode_kernel(pt_ref, lamq_ref, lamk_ref, subln_ref, qm_ref, kn_ref, vn_ref, *rest,
                        n_pp, lam_init):
    k_refs, v_refs = rest[:n_pp], rest[n_pp:2 * n_pp]
    o_ref, m_sc, l_sc, acc_sc = rest[2 * n_pp:]
    p_idx = pl.program_id(1)
    page, heads, _ = k_refs[0].shape[1:]
    n_rows = 2 * heads
    n_keys = page * heads

    @pl.when(p_idx == 0)
    def _():
        m_sc[...] = jnp.full(m_sc.shape, NEG, F32)
        l_sc[...] = jnp.zeros(l_sc.shape, F32)
        acc_sc[...] = jnp.zeros(acc_sc.shape, F32)

    qm = qm_ref[0]
    qm_bf = qm.astype(BF16)
    row = lax.broadcasted_iota(jnp.int32, (n_rows, n_keys), 0)
    col = lax.broadcasted_iota(jnp.int32, (n_rows, n_keys), 1)
    same_head = (col & (heads - 1)) == (row & (heads - 1))

    def online(s, pv_fn):
        m_prev = m_sc[...]
        m_new = jnp.maximum(m_prev, jnp.max(s, axis=-1, keepdims=True))
        corr = jnp.exp(m_prev - m_new)
        p = jnp.exp(s - m_new)
        l_sc[...] = l_sc[...] * corr + jnp.sum(p, axis=-1, keepdims=True)
        acc_sc[...] = acc_sc[...] * corr + pv_fn(p)
        m_sc[...] = m_new

    for j in range(n_pp):
        k2 = k_refs[j][0].reshape(n_keys, HEAD_COLS).astype(BF16)
        v2 = v_refs[j][0].reshape(n_keys, HEAD_COLS).astype(BF16)
        s = lax.dot_general(qm_bf, k2, (((1,), (1,)), ((), ())), preferred_element_type=F32)
        s = jnp.where(same_head, s, NEG)
        online(s, lambda p, v2=v2: jnp.dot(p.astype(BF16), v2, preferred_element_type=F32))

    @pl.when(p_idx == pl.num_programs(1) - 1)
    def _():
        k_new = jnp.concatenate([kn_ref[0], kn_ref[0]], axis=0)
        v_new = jnp.concatenate([vn_ref[0], vn_ref[0]], axis=0)
        s_new = jnp.sum(qm * k_new, axis=-1, keepdims=True)
        online(s_new, lambda p: p * v_new)
        w = acc_sc[...] / l_sc[...]
        lam = _diff_lambda(lamq_ref, lamk_ref, lam_init)
        o = w[:heads] - lam * w[heads:]
        o_ref[0] = (_rms(o, subln_ref[...]) * (1.0 - lam_init)).astype(o_ref.dtype)


def attn_decode(q, k_new, v_new, cache_k, cache_v, page_table, lam_q, lam_k, subln, lam_init, *, n_pp=4):
    bd, width = q.shape
    heads = width // HEAD_COLS
    assert heads == SUBLANES, "cache pages are viewed as (page*heads, 256) row tiles"
    n_pages = page_table.shape[1]
    n_pp = math.gcd(n_pp, n_pages)
    page = cache_k.shape[1]
    q4 = q.reshape(bd, heads, 2, QK_DIM).transpose(0, 2, 1, 3)
    qm = jnp.einsum('bmhd,mn->bmhnd', q4, jnp.eye(2, dtype=F32)).reshape(bd, 2 * heads, HEAD_COLS)
    small = lambda shape: pl.BlockSpec(shape, lambda b, p, pt: (0, 0))
    tok = lambda last: pl.BlockSpec((1,) + last, lambda b, p, pt: (b, 0, 0))

    def page_spec(j):
        return pl.BlockSpec((1, page, heads, HEAD_COLS), lambda b, p, pt: (pt[b, p * n_pp + j], 0, 0, 0))

    grid_spec = pltpu.PrefetchScalarGridSpec(
        num_scalar_prefetch=1,
        grid=(bd, n_pages // n_pp),
        in_specs=[small(lam_q.shape), small(lam_k.shape), small(subln.shape),
                  tok((2 * heads, HEAD_COLS)), tok((heads, HEAD_COLS)), tok((heads, HEAD_COLS))]
                 + [page_spec(j) for j in range(n_pp)] * 2,
        out_specs=tok((heads, HEAD_COLS)),
        scratch_shapes=[pltpu.VMEM((2 * heads, 1), F32), pltpu.VMEM((2 * heads, 1), F32),
                        pltpu.VMEM((2 * heads, HEAD_COLS), F32)])
    o = pl.pallas_call(
        functools.partial(_attn_decode_kernel, n_pp=n_pp, lam_init=lam_init),
        grid_spec=grid_spec,
        out_shape=jax.ShapeDtypeStruct((bd, heads, HEAD_COLS), BF16),
        compiler_params=_params("parallel", "arbitrary"),
    )(page_table, lam_q, lam_k, subln, qm, k_new.reshape(bd, heads, HEAD_COLS),
      v_new.reshape(bd, heads, HEAD_COLS), *([cache_k] * n_pp), *([cache_v] * n_pp))
    return o.reshape(bd, width)


def _ssm_input_kernel(lr_ref, li_ref, ldt_ref, br_ref, bi_ref, bbr_ref, bbi_ref):
    lr, li = lr_ref[...], li_ref[...]
    dt = jnp.exp(ldt_ref[...])
    mag = jnp.exp(dt * lr)
    a_re, a_im = mag * jnp.cos(dt * li), mag * jnp.sin(dt * li)
    den = lr * lr + li * li
    f_re = ((a_re - 1.0) * lr + a_im * li) / den
    f_im = (a_im * lr - (a_re - 1.0) * li) / den
    br, bi = br_ref[...], bi_ref[...]
    bbr_ref[...] = f_re * br - f_im * bi
    bbi_ref[...] = f_re * bi + f_im * br


def _ssm_power_kernel(lr_ref, li_ref, ldt_ref, pr_ref, pi_ref):
    steps = (lax.broadcasted_iota(jnp.int32, pr_ref.shape, 0) + 1).astype(F32)
    dt = jnp.exp(ldt_ref[...])
    mag = jnp.exp(steps * (dt * lr_ref[...]))
    ang = steps * (dt * li_ref[...])
    pr_ref[...] = mag * jnp.cos(ang)
    pi_ref[...] = mag * jnp.sin(ang)


def ssm_prepare(lam_re, lam_im, log_dt, b_re, b_im, c_re, c_im):
    groups, nst = lam_re.shape
    gps = SSM_SUPER // SSM_GROUP
    n_super = groups // gps
    rep = lambda a: jnp.repeat(a, SSM_GROUP, axis=0)
    ldt_gn = jnp.broadcast_to(log_dt[:, None], (groups, nst))
    full = lambda a: pl.BlockSpec(a.shape, lambda: (0,) * a.ndim)
    bt = lambda b: b.transpose(0, 2, 1).reshape(groups * SSM_GROUP, nst)
    ins = [rep(lam_re), rep(lam_im), rep(ldt_gn), bt(b_re), bt(b_im)]
    bbr, bbi = pl.pallas_call(
        _ssm_input_kernel,
        in_specs=[full(a) for a in ins], out_specs=[full(ins[0])] * 2,
        out_shape=[jax.ShapeDtypeStruct(ins[0].shape, F32)] * 2,
    )(*ins)
    flat = lambda a: a.reshape(1, groups * nst)
    ins = [flat(lam_re), flat(lam_im), flat(ldt_gn)]
    pw = jax.ShapeDtypeStruct((SSM_SEG, groups * nst), F32)
    p_re, p_im = pl.pallas_call(
        _ssm_power_kernel,
        in_specs=[full(a) for a in ins], out_specs=[pl.BlockSpec(pw.shape, lambda: (0, 0))] * 2,
        out_shape=[pw, pw],
    )(*ins)
    eye = jnp.eye(gps, dtype=F32)

    def in_blocks(bb):
        x = bb.reshape(n_super, gps, SSM_GROUP, nst)
        return jnp.einsum('kgsn,gh->kgshn', x, eye).reshape(n_super, SSM_SUPER, gps * nst)

    def out_blocks(c):
        x = c.reshape(n_super, gps, SSM_GROUP, nst)
        return jnp.einsum('kgsn,gh->kgnhs', x, eye).reshape(n_super, gps * nst, SSM_SUPER).astype(BF16)

    wb = jnp.concatenate([in_blocks(bbr), in_blocks(bbi)], axis=-1).astype(BF16)
    return dict(wb=wb, wc_re=out_blocks(c_re), wc_im=out_blocks(c_im), p_re=p_re, p_im=p_im)


def _gelu_tanh(x):
    return 0.5 * x * (1.0 + jnp.tanh(math.sqrt(2.0 / math.pi) * (x + 0.044715 * (x * x * x))))


def _ssm_prompt_kernel(u_ref, wb_ref, wcr_ref, wci_ref, pr_ref, pi_ref, d_ref, y_ref, sre_ref, sim_ref,
                       perm_sc, bu_sc, xb_sc, car_sc):
    c = pl.program_id(2)
    seg = pr_ref.shape[0]
    nst = pr_ref.shape[1]
    re, im = slice(0, nst), slice(nst, 2 * nst)

    @pl.when(c == 0)
    def _():
        car_sc[...] = jnp.zeros(car_sc.shape, F32)

    n_lane_tiles = perm_sc.shape[0]
    for s in range(SUBLANES):
        for lt in range(n_lane_tiles):
            perm_sc[lt, pl.ds(s, seg, stride=SUBLANES), :] = (
                u_ref[0, s * seg:(s + 1) * seg, lt * LANES:(lt + 1) * LANES])
    up = jnp.concatenate([perm_sc[lt] for lt in range(n_lane_tiles)], axis=-1)
    bu_sc[...] = jnp.dot(up.astype(BF16), wb_ref[0], preferred_element_type=F32)

    a_re = jnp.broadcast_to(pr_ref[0:1, :], (SUBLANES, nst))
    a_im = jnp.broadcast_to(pi_ref[0:1, :], (SUBLANES, nst))

    def scan_body(i, carry):
        xr, xi = carry
        rows = pl.ds(pl.multiple_of(i * SUBLANES, SUBLANES), SUBLANES)
        nr = a_re * xr - a_im * xi + bu_sc[rows, re]
        ni = a_re * xi + a_im * xr + bu_sc[rows, im]
        bu_sc[rows, re] = nr
        bu_sc[rows, im] = ni
        return nr, ni

    zero = jnp.zeros((SUBLANES, nst), F32)
    er, ei = lax.fori_loop(0, seg, scan_body, (zero, zero), unroll=4)

    s_re, s_im = pr_ref[seg - 1:seg, :], pi_ref[seg - 1:seg, :]
    hr, hi = car_sc[0:1, :], car_sc[1:2, :]
    starts_r, starts_i = [], []
    for s in range(SUBLANES):
        starts_r.append(hr)
        starts_i.append(hi)
        hr, hi = (s_re * hr - s_im * hi + er[s:s + 1], s_re * hi + s_im * hr + ei[s:s + 1])
    car_sc[0:1, :] = hr
    car_sc[1:2, :] = hi
    h0r = jnp.concatenate(starts_r, axis=0)
    h0i = jnp.concatenate(starts_i, axis=0)

    def fix_body(i2, _):
        halves_r, halves_i = [], []
        for half in range(2):
            i = i2 * 2 + half
            rows = pl.ds(pl.multiple_of(i * SUBLANES, SUBLANES), SUBLANES)
            pr, pi = pr_ref[pl.ds(i, 1), :], pi_ref[pl.ds(i, 1), :]
            halves_r.append(bu_sc[rows, re] + (pr * h0r - pi * h0i))
            halves_i.append(bu_sc[rows, im] + (pr * h0i + pi * h0r))
        rows16 = pl.ds(pl.multiple_of(i2 * 2 * SUBLANES, 2 * SUBLANES), 2 * SUBLANES)
        xb_sc[rows16, re] = jnp.concatenate(halves_r, axis=0).astype(BF16)
        xb_sc[rows16, im] = jnp.concatenate(halves_i, axis=0).astype(BF16)
        return 0

    lax.fori_loop(0, seg // 2, fix_body, 0, unroll=2)

    y = (jnp.dot(xb_sc[:, re], wcr_ref[0], preferred_element_type=F32)
         - jnp.dot(xb_sc[:, im], wci_ref[0], preferred_element_type=F32)
         + d_ref[...] * up)
    y = _gelu_tanh(y)
    for lt in range(n_lane_tiles):
        perm_sc[lt] = y[:, lt * LANES:(lt + 1) * LANES]
    for s in range(SUBLANES):
        for lt in range(n_lane_tiles):
            y_ref[0, s * seg:(s + 1) * seg, lt * LANES:(lt + 1) * LANES] = (
                perm_sc[lt, pl.ds(s, seg, stride=SUBLANES), :])

    @pl.when(c == pl.num_programs(2) - 1)
    def _():
        sre_ref[0] = hr
        sim_ref[0] = hi


def ssm_prompt(u, prm, d):
    bsz, s_len, width = u.shape
    n_super = width // SSM_SUPER
    nst = prm['p_re'].shape[1] // n_super
    chunk = SUBLANES * SSM_SEG
    blk = pl.BlockSpec((1, chunk, SSM_SUPER), lambda b, k, c: (b, c, k))
    st = pl.BlockSpec((1, 1, nst), lambda b, k, c: (b, 0, k))
    pw = pl.BlockSpec((SSM_SEG, nst), lambda b, k, c: (0, k))
    y, sre, sim = pl.pallas_call(
        _ssm_prompt_kernel,
        grid=(bsz, n_super, s_len // chunk),
        in_specs=[blk,
                  pl.BlockSpec((1, SSM_SUPER, 2 * nst), lambda b, k, c: (k, 0, 0)),
                  pl.BlockSpec((1, nst, SSM_SUPER), lambda b, k, c: (k, 0, 0)),
                  pl.BlockSpec((1, nst, SSM_SUPER), lambda b, k, c: (k, 0, 0)),
                  pw, pw,
                  pl.BlockSpec((1, SSM_SUPER), lambda b, k, c: (0, k))],
        out_specs=[blk, st, st],
        out_shape=[jax.ShapeDtypeStruct(u.shape, F32),
                   jax.ShapeDtypeStruct((bsz, 1, n_super * nst), F32),
                   jax.ShapeDtypeStruct((bsz, 1, n_super * nst), F32)],
        scratch_shapes=[pltpu.VMEM((SSM_SUPER // LANES, chunk, LANES), F32), pltpu.VMEM((chunk, 2 * nst), F32),
                        pltpu.VMEM((chunk, 2 * nst), BF16), pltpu.VMEM((2, nst), F32)],
        compiler_params=_params("parallel", "parallel", "arbitrary"),
    )(u, prm['wb'], prm['wc_re'], prm['wc_im'], prm['p_re'], prm['p_im'], d)
    return y, sre[:, 0], sim[:, 0]


def _ssm_step_kernel(u_ref, x0r_ref, x0i_ref, wb_ref, wcr_ref, wci_ref, pr_ref, pi_ref, d_ref,
                     y_ref, sre_ref, sim_ref):
    nst = x0r_ref.shape[1]
    u = u_ref[...]
    bu = jnp.dot(u.astype(BF16), wb_ref[0], preferred_element_type=F32)
    a_re, a_im = pr_ref[0:1, :], pi_ref[0:1, :]
    x0r, x0i = x0r_ref[...], x0i_ref[...]
    xr = a_re * x0r - a_im * x0i + bu[:, :nst]
    xi = a_re * x0i + a_im * x0r + bu[:, nst:]
    y = (jnp.dot(xr.astype(BF16), wcr_ref[0], preferred_element_type=F32)
         - jnp.dot(xi.astype(BF16), wci_ref[0], preferred_element_type=F32)
         + d_ref[...] * u)
    y_ref[...] = _gelu_tanh(y)
    sre_ref[...] = xr
    sim_ref[...] = xi


def ssm_step(u, x0_re, x0_im, prm, d):
    bd, width = u.shape
    n_super = width // SSM_SUPER
    nst = x0_re.shape[1] // n_super
    ub = pl.BlockSpec((bd, SSM_SUPER), lambda k: (0, k))
    st = pl.BlockSpec((bd, nst), lambda k: (0, k))
    pw = pl.BlockSpec((SUBLANES, nst), lambda k: (0, k))
    return pl.pallas_call(
        _ssm_step_kernel,
        grid=(n_super,),
        in_specs=[ub, st, st,
                  pl.BlockSpec((1, SSM_SUPER, 2 * nst), lambda k: (k, 0, 0)),
                  pl.BlockSpec((1, nst, SSM_SUPER), lambda k: (k, 0, 0)),
                  pl.BlockSpec((1, nst, SSM_SUPER), lambda k: (k, 0, 0)),
                  pw, pw,
                  pl.BlockSpec((1, SSM_SUPER), lambda k: (0, k))],
        out_specs=[ub, st, st],
        out_shape=[jax.ShapeDtypeStruct(u.shape, F32), jax.ShapeDtypeStruct(x0_re.shape, F32),
                   jax.ShapeDtypeStruct(x0_im.shape, F32)],
        compiler_params=_params("parallel"),
    )(u, x0_re, x0_im, prm['wb'], prm['wc_re'], prm['wc_im'], prm['p_re'], prm['p_im'], d)


def _run_layer(x, mod, wts, attend, ssm_fn):
    g, t, d = x.shape
    m = g * t
    md = lambda sub, kind: mod[:, :, sub * 3 + kind, :]
    flat = lambda a: a.reshape(m, a.shape[-1])
    gpre, gpost = wts['norm_pre'], wts['norm_post']
    att_w = wts['w_branch_attn'].shape[0]
    ssm_w = wts['w_branch_ssm'].shape[0]

    def ffn(h, gate_w, up_w, down_w):
        act = matmul_swiglu(flat(h), gate_w, up_w)
        return matmul_kgrid(act, down_w, tk=down_w.shape[0] // 4).reshape(g, t, d)

    h = modulate(x, gpre[0:1], md(0, 0), md(0, 1))
    y = ffn(h, wts['ffn1_gate'], wts['ffn1_up'], wts['ffn1_down'])
    x, h = resid_update(x, y, gpost[0:1], md(0, 2), 0.5, (gpre[1:2], md(1, 0), md(1, 1)))

    hf = flat(h)
    w_in = wts['w_in']
    (q,) = matmul_cols(hf, w_in, 0, att_w, [F32 if attend.wants_f32 else BF16], scale=QK_DIM ** -0.5)
    k32, k16 = matmul_cols(hf, w_in, att_w, att_w, [F32, BF16])
    v32, v16 = matmul_cols(hf, w_in, 2 * att_w, att_w, [F32, BF16])
    (u,) = matmul_cols(hf, w_in, 3 * att_w, ssm_w, [F32])
    (g_att,) = matmul_cols(hf, w_in, 3 * att_w + ssm_w, d, [BF16])
    (g_ssm,) = matmul_cols(hf, w_in, 3 * att_w + ssm_w + d, d, [BF16])

    o = attend(q, k32, k16, v32, v16)
    y_s, s_re, s_im = ssm_fn(u)
    y_s = matmul_glu(flat(y_s), wts['ssm_w_glu'])
    merged = matmul_merge(o, wts['w_branch_attn'], g_att, y_s, wts['w_branch_ssm'], g_ssm)
    (y,) = matmul_cols(merged, wts['w_out'], 0, d, [F32])
    x, h = resid_update(x, y.reshape(g, t, d), gpost[1:2], md(1, 2), 1.0, (gpre[2:3], md(2, 0), md(2, 1)))

    y = ffn(h, wts['ffn2_gate'], wts['ffn2_up'], wts['ffn2_down'])
    x, _ = resid_update(x, y, gpost[2:3], md(2, 2), 0.5)
    return x, k32, v32, s_re, s_im


class _Attend:
    def __init__(self, fn, wants_f32):
        self.fn, self.wants_f32 = fn, wants_f32

    def __call__(self, *a):
        return self.fn(*a)


def kernel(x_prompt, x_sample, cache_k, cache_v, state_ssm_re, state_ssm_im, page_table, c_prompt, c_sample, w_mod, b_mod, norm_pre, norm_post, ffn1_gate, ffn1_up, ffn1_down, w_in, lam_q, lam_k, attn_subln, w_branch_attn, ssm_lam_re, ssm_lam_im, ssm_log_dt, ssm_b_re, ssm_b_im, ssm_c_re, ssm_c_im, ssm_d, ssm_w_glu, w_branch_ssm, w_out, ffn2_gate, ffn2_up, ffn2_down):
    depth = w_mod.shape[0]
    bsz, s_len, d = x_prompt.shape
    bd, dec_seq, _ = x_sample.shape
    assert dec_seq == 1, "the decode attention handles one new token per sequence"
    heads = cache_k.shape[3]
    d_ff = ffn1_gate.shape[2]
    d_ff_pad = _round_up(d_ff, 1024)
    groups, nst = ssm_lam_re.shape[1:]

    xp, xs = x_prompt, x_sample.reshape(1, bd, d)
    outs = [[] for _ in range(8)]
    for li in range(depth):
        lam_init = 0.8 - 0.6 * math.exp(-0.3 * li)
        bf = lambda w: w[li].astype(BF16)
        pad_cols = lambda w: jnp.pad(bf(w), ((0, 0), (0, d_ff_pad - d_ff)))
        pad_rows = lambda w: jnp.pad(bf(w), ((0, d_ff_pad - d_ff), (0, 0)))
        wts = dict(norm_pre=norm_pre[li], norm_post=norm_post[li],
                   ffn1_gate=pad_cols(ffn1_gate), ffn1_up=pad_cols(ffn1_up), ffn1_down=pad_rows(ffn1_down),
                   ffn2_gate=pad_cols(ffn2_gate), ffn2_up=pad_cols(ffn2_up), ffn2_down=pad_rows(ffn2_down),
                   w_in=bf(w_in), w_branch_attn=bf(w_branch_attn), w_branch_ssm=bf(w_branch_ssm),
                   ssm_w_glu=bf(ssm_w_glu), w_out=bf(w_out))
        ssm_prm = ssm_prepare(ssm_lam_re[li], ssm_lam_im[li], ssm_log_dt[li], ssm_b_re[li], ssm_b_im[li],
                              ssm_c_re[li], ssm_c_im[li])
        d_row = ssm_d[li].reshape(1, -1)
        subln = attn_subln[li].reshape(1, -1)

        n_cond = bsz + bd
        c_all = jnp.pad(jnp.concatenate([c_prompt, c_sample], axis=0), ((0, _round_up(n_cond, 16) - n_cond), (0, 0)))
        mod = matmul_mod(c_all, w_mod[li], b_mod[li].reshape(1, -1))
        mod_p = mod[:bsz].reshape(bsz, 1, N_SUB * 3, d)
        mod_s = mod[bsz:n_cond].reshape(1, bd, N_SUB * 3, d)

        def attend_p(q, k32, k16, v32, v16):
            return attn_prompt(q, k16, v16, lam_q[li], lam_k[li], subln, bsz, lam_init)

        def attend_s(q, k32, k16, v32, v16):
            return attn_decode(q, k32, v32, cache_k[li], cache_v[li], page_table, lam_q[li], lam_k[li],
                               subln, lam_init)

        def ssm_p(u):
            y, s_re, s_im = ssm_prompt(u.reshape(bsz, s_len, -1), ssm_prm, d_row)
            return y, s_re, s_im

        def ssm_s(u):
            return ssm_step(u, state_ssm_re[li].reshape(bd, -1), state_ssm_im[li].reshape(bd, -1), ssm_prm, d_row)

        xp, k1, v1, r1, i1 = _run_layer(xp, mod_p, wts, _Attend(attend_p, False), ssm_p)
        xs, k2, v2, r2, i2 = _run_layer(xs, mod_s, wts, _Attend(attend_s, True), ssm_s)
        new = [k1.reshape(bsz, s_len, heads, HEAD_COLS), v1.reshape(bsz, s_len, heads, HEAD_COLS),
               r1.reshape(bsz, groups, nst), i1.reshape(bsz, groups, nst),
               k2.reshape(bd, 1, heads, HEAD_COLS), v2.reshape(bd, 1, heads, HEAD_COLS),
               r2.reshape(bd, groups, nst), i2.reshape(bd, groups, nst)]
        for acc, val in zip(outs, new):
            acc.append(val)
    return (xp, xs.reshape(bd, 1, d), *[jnp.stack(o) for o in outs])
```

```python
import functools
import math

import numpy as np
import jax
import jax.numpy as jnp
from jax import lax
from jax.experimental import pallas as pl
from jax.experimental.pallas import tpu as pltpu

F32 = jnp.float32
BF16 = jnp.bfloat16
EPS = 1e-6
NEG = -1e30

LOG2_E = math.log2(math.e)
QK_DIM = 128
HEAD_COLS = 2 * QK_DIM
SSM_GROUP = 16
SSM_STATE = 64
N_SUB = 3
SUBLANES = 8
LANES = 128
SSM_SUPER = 256
SSM_SEG = 64
VMEM_LIMIT_BYTES = 56 * 1024 * 1024


def _params(*sem):
    return pltpu.CompilerParams(dimension_semantics=sem, vmem_limit_bytes=VMEM_LIMIT_BYTES)


def _round_up(x, m):
    return (x + m - 1) // m * m


def _sigmoid(x):
    return 1.0 / (1.0 + jnp.exp(-x))


def _rms(x, gain):
    return x * lax.rsqrt(jnp.mean(x * x, axis=-1, keepdims=True) + EPS) * gain


def _mm_segments_kernel(x_ref, w_ref, *o_refs, segs):
    j = pl.program_id(1)
    pos = 0
    for start, count, scale, dtypes in segs:
        outs = o_refs[pos:pos + len(dtypes)]
        pos += len(dtypes)

        @pl.when(jnp.logical_and(j >= start, j < start + count))
        def _(outs=outs, scale=scale):
            acc = jnp.dot(x_ref[...], w_ref[...], preferred_element_type=F32)
            if scale != 1.0:
                acc = acc * scale
            for o_ref in outs:
                o_ref[...] = acc.astype(o_ref.dtype)


def matmul_segments(x, w, segments, *, tm=1024, tn=512):
    m, k = x.shape
    tm = min(tm, m)
    segs, in_tiles, out_specs, out_shape = [], 0, [], []
    for n_cols, scale, dtypes in segments:
        start, count = in_tiles, n_cols // tn
        segs.append((start, count, scale, tuple(dtypes)))
        in_tiles += count
        for dt in dtypes:
            out_specs.append(pl.BlockSpec(
                (tm, tn), lambda i, j, start=start, count=count: (i, jnp.clip(j - start, 0, count - 1))))
            out_shape.append(jax.ShapeDtypeStruct((m, n_cols), dt))
    assert in_tiles * tn == w.shape[1]
    return pl.pallas_call(
        functools.partial(_mm_segments_kernel, segs=tuple(segs)),
        grid=(m // tm, in_tiles),
        in_specs=[pl.BlockSpec((tm, k), lambda i, j: (i, 0)),
                  pl.BlockSpec((k, tn), lambda i, j: (0, j))],
        out_specs=out_specs, out_shape=out_shape,
        compiler_params=_params("arbitrary", "arbitrary"),
        name="w_in_segments",
    )(x, w)


def _mm_ws_kernel(x_ref, w_ref, o_ref, w_sc):
    _cast_weights_once([(w_ref, w_sc)], pl.program_id(1) == 0)
    o_ref[...] = jnp.dot(x_ref[...], w_sc[...], preferred_element_type=F32).astype(o_ref.dtype)


def matmul_ws(x, w, out_dtype, *, tm=1024, tn=512, name):
    m, k = x.shape
    n = w.shape[1]
    tm, tn = min(tm, m), min(tn, n)
    return pl.pallas_call(
        _mm_ws_kernel,
        grid=(n // tn, m // tm),
        in_specs=[pl.BlockSpec((tm, k), lambda j, i: (i, 0)),
                  pl.BlockSpec((k, tn), lambda j, i: (0, j))],
        out_specs=pl.BlockSpec((tm, tn), lambda j, i: (i, j)),
        out_shape=jax.ShapeDtypeStruct((m, n), out_dtype),
        scratch_shapes=[pltpu.VMEM((k, tn), BF16)],
        compiler_params=_params("arbitrary", "arbitrary"),
        name=name,
    )(x, w)


def _cast_pad_rows_kernel(w_ref, o_ref, *, n_rows):
    tr = w_ref.shape[0]
    row = lax.broadcasted_iota(jnp.int32, w_ref.shape, 0) + pl.program_id(0) * tr
    o_ref[...] = jnp.where(row < n_rows, w_ref[...], 0.0).astype(o_ref.dtype)


def cast_pad_rows(w, n_rows_out, *, tr=512):
    r, c = w.shape
    return pl.pallas_call(
        functools.partial(_cast_pad_rows_kernel, n_rows=r),
        grid=(n_rows_out // tr,),
        in_specs=[pl.BlockSpec((tr, c), lambda i: (jnp.minimum(i, pl.cdiv(r, tr) - 1), 0))],
        out_specs=pl.BlockSpec((tr, c), lambda i: (i, 0)),
        out_shape=jax.ShapeDtypeStruct((n_rows_out, c), BF16),
        compiler_params=_params("parallel"),
        name="cast_pad_rows",
    )(w)


def _mm_kgrid_kernel(x_ref, w_ref, o_ref):
    part = jnp.dot(x_ref[...], w_ref[...], preferred_element_type=F32)

    @pl.when(pl.program_id(2) == 0)
    def _():
        o_ref[...] = part

    @pl.when(pl.program_id(2) != 0)
    def _():
        o_ref[...] += part


def matmul_kgrid(x, w, *, tm=1024, tn=1024, tk):
    m, k = x.shape
    n = w.shape[1]
    tm, tn = min(tm, m), min(tn, n)
    return pl.pallas_call(
        _mm_kgrid_kernel,
        grid=(m // tm, n // tn, k // tk),
        in_specs=[pl.BlockSpec((tm, tk), lambda i, j, kk: (i, kk)),
                  pl.BlockSpec((tk, tn), lambda i, j, kk: (kk, j))],
        out_specs=pl.BlockSpec((tm, tn), lambda i, j, kk: (i, j)),
        out_shape=jax.ShapeDtypeStruct((m, n), F32),
        compiler_params=_params("parallel", "parallel", "arbitrary"),
        name="ffn_down",
    )(x, w)


def _cast_weights_once(pairs, do_cast):
    @pl.when(do_cast)
    def _():
        for src, dst in pairs:
            dst[...] = src[...].astype(BF16)


def _mm_swiglu_kernel(x_ref, wg_ref, wu_ref, o_ref, wg_sc, wu_sc, *, n_real):
    j, i = pl.program_id(0), pl.program_id(1)
    _cast_weights_once([(wg_ref, wg_sc), (wu_ref, wu_sc)], jnp.logical_and(i == 0, j < n_real))

    @pl.when(j < n_real)
    def _():
        x = x_ref[...]
        g = jnp.dot(x, wg_sc[...], preferred_element_type=F32)
        u = jnp.dot(x, wu_sc[...], preferred_element_type=F32)
        o_ref[...] = (g * _sigmoid(g) * u).astype(o_ref.dtype)

    @pl.when(j >= n_real)
    def _():
        o_ref[...] = jnp.zeros(o_ref.shape, o_ref.dtype)


def matmul_swiglu(x, wg, wu, n_out, *, tm=1024, tn=256):
    m, k = x.shape
    n = wg.shape[1]
    tm = min(tm, m)
    n_real = n // tn
    w_spec = pl.BlockSpec((k, tn), lambda j, i: (0, jnp.minimum(j, n_real - 1)))
    return pl.pallas_call(
        functools.partial(_mm_swiglu_kernel, n_real=n_real),
        grid=(n_out // tn, m // tm),
        in_specs=[pl.BlockSpec((tm, k), lambda j, i: (jnp.where(j < n_real, i, 0), 0)), w_spec, w_spec],
        out_specs=pl.BlockSpec((tm, tn), lambda j, i: (i, j)),
        out_shape=jax.ShapeDtypeStruct((m, n_out), BF16),
        scratch_shapes=[pltpu.VMEM((k, tn), BF16), pltpu.VMEM((k, tn), BF16)],
        compiler_params=_params("arbitrary", "arbitrary"),
        name="ffn_up_swiglu",
    )(x, wg, wu)


def _mm_glu_kernel(x_ref, w_ref, e_ref, o_ref):
    acc = jnp.dot(x_ref[...].astype(BF16), w_ref[...], preferred_element_type=F32)
    o_ref[...] = (e_ref[...] * _sigmoid(acc)).astype(o_ref.dtype)


def matmul_glu(y, w, *, tm=1024, tn=512):
    m, k = y.shape
    tm, tn = min(tm, m), min(tn, k)
    return pl.pallas_call(
        _mm_glu_kernel,
        grid=(m // tm, k // tn),
        in_specs=[pl.BlockSpec((tm, k), lambda i, j: (i, 0)),
                  pl.BlockSpec((k, tn), lambda i, j: (0, j)),
                  pl.BlockSpec((tm, tn), lambda i, j: (i, j))],
        out_specs=pl.BlockSpec((tm, tn), lambda i, j: (i, j)),
        out_shape=jax.ShapeDtypeStruct((m, k), BF16),
        compiler_params=_params("parallel", "arbitrary"),
        name="ssm_glu",
    )(y, w, y)


def _mm_merge_kernel(xa_ref, wa_ref, ga_ref, xs_ref, ws_ref, gs_ref, o_ref, wa_sc, ws_sc):
    _cast_weights_once([(wa_ref, wa_sc), (ws_ref, ws_sc)], pl.program_id(1) == 0)
    ya = jnp.dot(xa_ref[...], wa_sc[...], preferred_element_type=F32)
    ys = jnp.dot(xs_ref[...], ws_sc[...], preferred_element_type=F32)
    merged = _sigmoid(ga_ref[...].astype(F32)) * ya + _sigmoid(gs_ref[...].astype(F32)) * ys
    o_ref[...] = merged.astype(o_ref.dtype)


def matmul_merge(xa, wa, ga, xs, ws, gs, *, tm=1024, tn=512):
    m, ka = xa.shape
    ks = xs.shape[1]
    n = wa.shape[1]
    tm, tn = min(tm, m), min(tn, n)
    tile = pl.BlockSpec((tm, tn), lambda j, i: (i, j))
    return pl.pallas_call(
        _mm_merge_kernel,
        grid=(n // tn, m // tm),
        in_specs=[pl.BlockSpec((tm, ka), lambda j, i: (i, 0)),
                  pl.BlockSpec((ka, tn), lambda j, i: (0, j)), tile,
                  pl.BlockSpec((tm, ks), lambda j, i: (i, 0)),
                  pl.BlockSpec((ks, tn), lambda j, i: (0, j)), tile],
        out_specs=tile,
        out_shape=jax.ShapeDtypeStruct((m, n), BF16),
        scratch_shapes=[pltpu.VMEM((ka, tn), BF16), pltpu.VMEM((ks, tn), BF16)],
        compiler_params=_params("arbitrary", "arbitrary"),
        name="branch_merge",
    )(xa, wa, ga, xs, ws, gs)


def _mm_mod_kernel(c_ref, w_ref, b_ref, o_ref):
    c = c_ref[...]
    lhs = (c * _sigmoid(c)).astype(BF16)
    acc = jnp.dot(lhs, w_ref[...].astype(BF16), preferred_element_type=F32)
    o_ref[...] = acc + b_ref[...]


def matmul_mod(c, w, b, *, tn=512):
    m, k = c.shape
    n = w.shape[1]
    return pl.pallas_call(
        _mm_mod_kernel,
        grid=(n // tn,),
        in_specs=[pl.BlockSpec((m, k), lambda j: (0, 0)),
                  pl.BlockSpec((k, tn), lambda j: (0, j)),
                  pl.BlockSpec((1, tn), lambda j: (0, j))],
        out_specs=pl.BlockSpec((m, tn), lambda j: (0, j)),
        out_shape=jax.ShapeDtypeStruct((m, n), F32),
        compiler_params=_params("arbitrary"),
        name="adaln_mod",
    )(c, w, b)


def _modulate_kernel(x_ref, g_ref, sh_ref, sc_ref, h_ref):
    h = _rms(x_ref[0], g_ref[...]) * (1.0 + sc_ref[0]) + sh_ref[0]
    h_ref[0] = h.astype(h_ref.dtype)


def _resid_kernel(x_ref, y_ref, gpost_ref, gate_ref, *rest, coef, with_next):
    xn = x_ref[0] + coef * gate_ref[0] * _rms(y_ref[0], gpost_ref[...])
    if with_next:
        gpre_ref, sh_ref, sc_ref, xo_ref, h_ref = rest
        h = _rms(xn, gpre_ref[...]) * (1.0 + sc_ref[0]) + sh_ref[0]
        h_ref[0] = h.astype(h_ref.dtype)
    else:
        (xo_ref,) = rest
    xo_ref[0] = xn


def _row_specs(x, mod_rows, tr):
    g, t, d = x.shape
    tr = min(tr, t)
    row = pl.BlockSpec((1, tr, d), lambda gi, ti: (gi, ti, 0))
    gain = pl.BlockSpec((1, d), lambda gi, ti: (0, 0))
    if mod_rows == 1:
        mod = pl.BlockSpec((1, 1, d), lambda gi, ti: (gi, 0, 0))
    else:
        mod = pl.BlockSpec((1, tr, d), lambda gi, ti: (gi, ti, 0))
    return (g, t // tr), row, gain, mod


def modulate(x, gain, shift, scale, *, tr=256):
    grid, row, gspec, mod = _row_specs(x, shift.shape[1], tr)
    return pl.pallas_call(
        _modulate_kernel, grid=grid,
        in_specs=[row, gspec, mod, mod], out_specs=row,
        out_shape=jax.ShapeDtypeStruct(x.shape, BF16),
        compiler_params=_params("parallel", "parallel"),
        name="modulate",
    )(x, gain, shift, scale)


def resid_update(x, y, gpost, gate, coef, nxt=None, *, tr=256):
    grid, row, gspec, mod = _row_specs(x, gate.shape[1], tr)
    in_specs = [row, row, gspec, mod]
    args = [x, y, gpost, gate]
    out_specs = [row]
    out_shape = [jax.ShapeDtypeStruct(x.shape, F32)]
    if nxt is not None:
        in_specs += [gspec, mod, mod]
        args += list(nxt)
        out_specs.append(row)
        out_shape.append(jax.ShapeDtypeStruct(x.shape, BF16))
    outs = pl.pallas_call(
        functools.partial(_resid_kernel, coef=coef, with_next=nxt is not None), grid=grid,
        in_specs=in_specs, out_specs=out_specs, out_shape=out_shape,
        compiler_params=_params("parallel", "parallel"),
        name="resid_update",
    )(*args)
    return outs if nxt is not None else (outs[0], None)


def _diff_lambda(lamq_ref, lamk_ref, lam_init):
    prod = lamq_ref[...] * lamk_ref[...]
    s0 = jnp.sum(prod[0:1], axis=-1, keepdims=True)
    s1 = jnp.sum(prod[1:2], axis=-1, keepdims=True)
    return jnp.exp(s0) - jnp.exp(s1) + lam_init


def _attn_prompt_kernel(qi_tab, ki_tab, lamq_ref, lamk_ref, subln_ref, q_ref, k_ref, v_ref, o_ref,
                        m_sc, l_sc, acc_sc, *, lam_init):
    t = pl.program_id(2)
    qi, ki = qi_tab[t], ki_tab[t]
    tq, tk = q_ref.shape[0], k_ref.shape[0]

    @pl.when(ki == 0)
    def _():
        m_sc[...] = jnp.full(m_sc.shape, NEG, F32)
        l_sc[...] = jnp.zeros(l_sc.shape, F32)
        acc_sc[...] = jnp.zeros(acc_sc.shape, F32)

    def step(diagonal):
        q, k, v = q_ref[...], k_ref[...], v_ref[...]
        for mp in range(2):
            cols = slice(mp * QK_DIM, (mp + 1) * QK_DIM)
            s = lax.dot_general(q[:, cols], k[:, cols], (((1,), (1,)), ((), ())),
                                preferred_element_type=F32)
            if diagonal:
                row = lax.broadcasted_iota(jnp.int32, (tq, tk), 0)
                col = lax.broadcasted_iota(jnp.int32, (tq, tk), 1)
                s = jnp.where(col <= row, s, NEG)
            m_prev = m_sc[mp]
            m_new = jnp.maximum(m_prev, jnp.max(s, axis=-1, keepdims=True))
            corr = jnp.exp2(m_prev - m_new)
            p = jnp.exp2(s - m_new)
            l_sc[mp] = l_sc[mp] * corr + jnp.sum(p, axis=-1, keepdims=True)
            acc_sc[mp] = acc_sc[mp] * corr + jnp.dot(p.astype(BF16), v, preferred_element_type=F32)
            m_sc[mp] = m_new

    @pl.when(ki < qi)
    def _():
        step(False)

    @pl.when(ki == qi)
    def _():
        step(True)
        lam = _diff_lambda(lamq_ref, lamk_ref, lam_init)
        o = acc_sc[0] / l_sc[0] - lam * (acc_sc[1] / l_sc[1])
        o_ref[...] = (_rms(o, subln_ref[...]) * (1.0 - lam_init)).astype(o_ref.dtype)


def attn_prompt(q, k, v, lam_q, lam_k, subln, bsz, lam_init, *, tq=512):
    m, width = q.shape
    s_len = m // bsz
    heads = width // HEAD_COLS
    tq = min(tq, s_len)
    nq = s_len // tq
    pairs = [(a, b) for a in range(nq) for b in range(a + 1)]
    qi_tab = jnp.asarray(np.array([p[0] for p in pairs], np.int32))
    ki_tab = jnp.asarray(np.array([p[1] for p in pairs], np.int32))
    small = lambda shape: pl.BlockSpec(shape, lambda b, h, t, qt, kt: (0, 0))
    grid_spec = pltpu.PrefetchScalarGridSpec(
        num_scalar_prefetch=2,
        grid=(bsz, heads, len(pairs)),
        in_specs=[small(lam_q.shape), small(lam_k.shape), small(subln.shape),
                  pl.BlockSpec((tq, HEAD_COLS), lambda b, h, t, qt, kt: (b * nq + qt[t], h)),
                  pl.BlockSpec((tq, HEAD_COLS), lambda b, h, t, qt, kt: (b * nq + kt[t], h)),
                  pl.BlockSpec((tq, HEAD_COLS), lambda b, h, t, qt, kt: (b * nq + kt[t], h))],
        out_specs=pl.BlockSpec((tq, HEAD_COLS), lambda b, h, t, qt, kt: (b * nq + qt[t], h)),
        scratch_shapes=[pltpu.VMEM((2, tq, 1), F32), pltpu.VMEM((2, tq, 1), F32),
                        pltpu.VMEM((2, tq, HEAD_COLS), F32)])
    return pl.pallas_call(
        functools.partial(_attn_prompt_kernel, lam_init=lam_init),
        grid_spec=grid_spec,
        out_shape=jax.ShapeDtypeStruct((m, width), BF16),
        compiler_params=_params("parallel", "parallel", "arbitrary"),
        name="attn_prompt",
    )(qi_tab, ki_tab, lam_q, lam_k, subln, q, k, v)


def _attn_decode_kernel(pt_ref, lamq_ref, lamk_ref, subln_ref, qm_ref, kn_ref, vn_ref, *rest,
                        n_pp, lam_init):
    k_refs, v_refs = rest[:n_pp], rest[n_pp:2 * n_pp]
    o_ref, m_sc, l_sc, acc_sc = rest[2 * n_pp:]
    p_idx = pl.program_id(1)
    page, heads, _ = k_refs[0].shape[1:]
    n_rows = 2 * heads
    n_keys = page * heads

    @pl.when(p_idx == 0)
    def _():
        m_sc[...] = jnp.full(m_sc.shape, NEG, F32)
        l_sc[...] = jnp.zeros(l_sc.shape, F32)
        acc_sc[...] = jnp.zeros(acc_sc.shape, F32)

    qm = qm_ref[0]
    qm_bf = qm.astype(BF16)
    row = lax.broadcasted_iota(jnp.int32, (n_rows, n_keys), 0)
    col = lax.broadcasted_iota(jnp.int32, (n_rows, n_keys), 1)
    same_head = (col & (heads - 1)) == (row & (heads - 1))

    def online(scores, pv_fns):
        m_prev = m_sc[...]
        m_new = m_prev
        for s in scores:
            m_new = jnp.maximum(m_new, jnp.max(s, axis=-1, keepdims=True))
        corr = jnp.exp(m_prev - m_new)
        l_new = l_sc[...] * corr
        acc = acc_sc[...] * corr
        for s, pv_fn in zip(scores, pv_fns):
            p = jnp.exp(s - m_new)
            l_new = l_new + jnp.sum(p, axis=-1, keepdims=True)
            acc = acc + pv_fn(p)
        l_sc[...] = l_new
        acc_sc[...] = acc
        m_sc[...] = m_new

    scores, pv_fns = [], []
    for j in range(n_pp):
        k2 = k_refs[j][0].reshape(n_keys, HEAD_COLS).astype(BF16)
        s = lax.dot_general(qm_bf, k2, (((1,), (1,)), ((), ())), preferred_element_type=F32)
        scores.append(jnp.where(same_head, s, NEG))
        pv_fns.append(lambda p, j=j: jnp.dot(
            p.astype(BF16), v_refs[j][0].reshape(n_keys, HEAD_COLS).astype(BF16), preferred_element_type=F32))
    online(scores, pv_fns)

    @pl.when(p_idx == pl.num_programs(1) - 1)
    def _():
        k_new = jnp.concatenate([kn_ref[0], kn_ref[0]], axis=0)
        v_new = jnp.concatenate([vn_ref[0], vn_ref[0]], axis=0)
        s_new = jnp.sum(qm * k_new, axis=-1, keepdims=True)
        online([s_new], [lambda p: p * v_new])
        w = acc_sc[...] / l_sc[...]
        lam = _diff_lambda(lamq_ref, lamk_ref, lam_init)
        o = w[:heads] - lam * w[heads:]
        o_ref[0] = (_rms(o, subln_ref[...]) * (1.0 - lam_init)).astype(o_ref.dtype)


def attn_decode(q, k_new, v_new, cache_k, cache_v, page_table, lam_q, lam_k, subln, lam_init, *, n_pp=8):
    bd, width = q.shape
    heads = width // HEAD_COLS
    assert heads == SUBLANES, "cache pages are viewed as (page*heads, 256) row tiles"
    n_pages = page_table.shape[1]
    n_pp = math.gcd(n_pp, n_pages)
    page = cache_k.shape[1]
    q4 = q.reshape(bd, heads, 2, QK_DIM).transpose(0, 2, 1, 3)
    qm = jnp.einsum('bmhd,mn->bmhnd', q4, jnp.eye(2, dtype=F32)).reshape(bd, 2 * heads, HEAD_COLS)
    small = lambda shape: pl.BlockSpec(shape, lambda b, p, pt: (0, 0))
    tok = lambda last: pl.BlockSpec((1,) + last, lambda b, p, pt: (b, 0, 0))

    def page_spec(j):
        return pl.BlockSpec((1, page, heads, HEAD_COLS), lambda b, p, pt: (pt[b, p * n_pp + j], 0, 0, 0))

    grid_spec = pltpu.PrefetchScalarGridSpec(
        num_scalar_prefetch=1,
        grid=(bd, n_pages // n_pp),
        in_specs=[small(lam_q.shape), small(lam_k.shape), small(subln.shape),
                  tok((2 * heads, HEAD_COLS)), tok((heads, HEAD_COLS)), tok((heads, HEAD_COLS))]
                 + [page_spec(j) for j in range(n_pp)] * 2,
        out_specs=tok((heads, HEAD_COLS)),
        scratch_shapes=[pltpu.VMEM((2 * heads, 1), F32), pltpu.VMEM((2 * heads, 1), F32),
                        pltpu.VMEM((2 * heads, HEAD_COLS), F32)])
    o = pl.pallas_call(
        functools.partial(_attn_decode_kernel, n_pp=n_pp, lam_init=lam_init),
        grid_spec=grid_spec,
        out_shape=jax.ShapeDtypeStruct((bd, heads, HEAD_COLS), BF16),
        compiler_params=_params("parallel", "arbitrary"),
        name="attn_decode",
    )(page_table, lam_q, lam_k, subln, qm, k_new.reshape(bd, heads, HEAD_COLS),
      v_new.reshape(bd, heads, HEAD_COLS), *([cache_k] * n_pp), *([cache_v] * n_pp))
    return o.reshape(bd, width)


def _ssm_input_kernel(lr_ref, li_ref, ldt_ref, br_ref, bi_ref, bbr_ref, bbi_ref):
    lr, li = lr_ref[...], li_ref[...]
    dt = jnp.exp(ldt_ref[...])
    mag = jnp.exp(dt * lr)
    a_re, a_im = mag * jnp.cos(dt * li), mag * jnp.sin(dt * li)
    den = lr * lr + li * li
    f_re = ((a_re - 1.0) * lr + a_im * li) / den
    f_im = (a_im * lr - (a_re - 1.0) * li) / den
    br, bi = br_ref[...], bi_ref[...]
    bbr_ref[...] = f_re * br - f_im * bi
    bbi_ref[...] = f_re * bi + f_im * br


def _ssm_power_kernel(lr_ref, li_ref, ldt_ref, pr_ref, pi_ref):
    steps = (lax.broadcasted_iota(jnp.int32, pr_ref.shape, 0) + 1).astype(F32)
    dt = jnp.exp(ldt_ref[...])
    mag = jnp.exp(steps * (dt * lr_ref[...]))
    ang = steps * (dt * li_ref[...])
    pr_ref[...] = mag * jnp.cos(ang)
    pi_ref[...] = mag * jnp.sin(ang)


def ssm_prepare(lam_re, lam_im, log_dt, b_re, b_im, c_re, c_im):
    groups, nst = lam_re.shape
    gps = SSM_SUPER // SSM_GROUP
    n_super = groups // gps
    rep = lambda a: jnp.repeat(a, SSM_GROUP, axis=0)
    ldt_gn = jnp.broadcast_to(log_dt[:, None], (groups, nst))
    full = lambda a: pl.BlockSpec(a.shape, lambda: (0,) * a.ndim)
    bt = lambda b: b.transpose(0, 2, 1).reshape(groups * SSM_GROUP, nst)
    ins = [rep(lam_re), rep(lam_im), rep(ldt_gn), bt(b_re), bt(b_im)]
    bbr, bbi = pl.pallas_call(
        _ssm_input_kernel,
        in_specs=[full(a) for a in ins], out_specs=[full(ins[0])] * 2,
        out_shape=[jax.ShapeDtypeStruct(ins[0].shape, F32)] * 2,
    )(*ins)
    flat = lambda a: a.reshape(1, groups * nst)
    ins = [flat(lam_re), flat(lam_im), flat(ldt_gn)]
    pw = jax.ShapeDtypeStruct((SSM_SEG, groups * nst), F32)
    p_re, p_im = pl.pallas_call(
        _ssm_power_kernel,
        in_specs=[full(a) for a in ins], out_specs=[pl.BlockSpec(pw.shape, lambda: (0, 0))] * 2,
        out_shape=[pw, pw],
    )(*ins)
    eye = jnp.eye(gps, dtype=F32)

    def in_blocks(bb):
        x = bb.reshape(n_super, gps, SSM_GROUP, nst)
        return jnp.einsum('kgsn,gh->kgshn', x, eye).reshape(n_super, SSM_SUPER, gps * nst)

    def out_blocks(c):
        x = c.reshape(n_super, gps, SSM_GROUP, nst)
        return jnp.einsum('kgsn,gh->kgnhs', x, eye).reshape(n_super, gps * nst, SSM_SUPER).astype(BF16)

    wb = jnp.concatenate([in_blocks(bbr), in_blocks(bbi)], axis=-1).astype(BF16)
    return dict(wb=wb, wc_re=out_blocks(c_re), wc_im=out_blocks(c_im), p_re=p_re, p_im=p_im)


def _gelu_tanh(x):
    return 0.5 * x * (1.0 + jnp.tanh(math.sqrt(2.0 / math.pi) * (x + 0.044715 * (x * x * x))))


def _ssm_prompt_kernel(u_ref, wb_ref, wcr_ref, wci_ref, pr_ref, pi_ref, d_ref, y_ref, sre_ref, sim_ref,
                       perm_sc, bu_sc, xb_sc, car_sc):
    c = pl.program_id(2)
    seg = pr_ref.shape[0]
    nst = pr_ref.shape[1]
    re, im = slice(0, nst), slice(nst, 2 * nst)

    @pl.when(c == 0)
    def _():
        car_sc[...] = jnp.zeros(car_sc.shape, F32)

    n_lane_tiles = perm_sc.shape[0]
    for s in range(SUBLANES):
        for lt in range(n_lane_tiles):
            perm_sc[lt, pl.ds(s, seg, stride=SUBLANES), :] = (
                u_ref[0, s * seg:(s + 1) * seg, lt * LANES:(lt + 1) * LANES])
    up = jnp.concatenate([perm_sc[lt] for lt in range(n_lane_tiles)], axis=-1)
    bu_sc[...] = jnp.dot(up.astype(BF16), wb_ref[0], preferred_element_type=F32)

    a_re = jnp.broadcast_to(pr_ref[0:1, :], (SUBLANES, nst))
    a_im = jnp.broadcast_to(pi_ref[0:1, :], (SUBLANES, nst))

    def scan_body(i, carry):
        xr, xi = carry
        rows = pl.ds(pl.multiple_of(i * SUBLANES, SUBLANES), SUBLANES)
        nr = a_re * xr - a_im * xi + bu_sc[rows, re]
        ni = a_re * xi + a_im * xr + bu_sc[rows, im]
        bu_sc[rows, re] = nr
        bu_sc[rows, im] = ni
        return nr, ni

    zero = jnp.zeros((SUBLANES, nst), F32)
    er, ei = lax.fori_loop(0, seg, scan_body, (zero, zero), unroll=4)

    s_re, s_im = pr_ref[seg - 1:seg, :], pi_ref[seg - 1:seg, :]
    hr, hi = car_sc[0:1, :], car_sc[1:2, :]
    starts_r, starts_i = [], []
    for s in range(SUBLANES):
        starts_r.append(hr)
        starts_i.append(hi)
        hr, hi = (s_re * hr - s_im * hi + er[s:s + 1], s_re * hi + s_im * hr + ei[s:s + 1])
    car_sc[0:1, :] = hr
    car_sc[1:2, :] = hi
    h0r = jnp.concatenate(starts_r, axis=0)
    h0i = jnp.concatenate(starts_i, axis=0)

    def fix_body(i2, _):
        halves_r, halves_i = [], []
        for half in range(2):
            i = i2 * 2 + half
            rows = pl.ds(pl.multiple_of(i * SUBLANES, SUBLANES), SUBLANES)
            pr, pi = pr_ref[pl.ds(i, 1), :], pi_ref[pl.ds(i, 1), :]
            halves_r.append(bu_sc[rows, re] + (pr * h0r - pi * h0i))
            halves_i.append(bu_sc[rows, im] + (pr * h0i + pi * h0r))
        rows16 = pl.ds(pl.multiple_of(i2 * 2 * SUBLANES, 2 * SUBLANES), 2 * SUBLANES)
        xb_sc[rows16, re] = jnp.concatenate(halves_r, axis=0).astype(BF16)
        xb_sc[rows16, im] = jnp.concatenate(halves_i, axis=0).astype(BF16)
        return 0

    lax.fori_loop(0, seg // 2, fix_body, 0, unroll=2)

    y = (jnp.dot(xb_sc[:, re], wcr_ref[0], preferred_element_type=F32)
         - jnp.dot(xb_sc[:, im], wci_ref[0], preferred_element_type=F32)
         + d_ref[...] * up)
    y = _gelu_tanh(y)
    for lt in range(n_lane_tiles):
        perm_sc[lt] = y[:, lt * LANES:(lt + 1) * LANES]
    for s in range(SUBLANES):
        for lt in range(n_lane_tiles):
            y_ref[0, s * seg:(s + 1) * seg, lt * LANES:(lt + 1) * LANES] = (
                perm_sc[lt, pl.ds(s, seg, stride=SUBLANES), :])

    @pl.when(c == pl.num_programs(2) - 1)
    def _():
        sre_ref[0] = hr
        sim_ref[0] = hi


def ssm_prompt(u, prm, d):
    bsz, s_len, width = u.shape
    n_super = width // SSM_SUPER
    nst = prm['p_re'].shape[1] // n_super
    chunk = SUBLANES * SSM_SEG
    blk = pl.BlockSpec((1, chunk, SSM_SUPER), lambda b, k, c: (b, c, k))
    st = pl.BlockSpec((1, 1, nst), lambda b, k, c: (b, 0, k))
    pw = pl.BlockSpec((SSM_SEG, nst), lambda b, k, c: (0, k))
    y, sre, sim = pl.pallas_call(
        _ssm_prompt_kernel,
        grid=(bsz, n_super, s_len // chunk),
        in_specs=[blk,
                  pl.BlockSpec((1, SSM_SUPER, 2 * nst), lambda b, k, c: (k, 0, 0)),
                  pl.BlockSpec((1, nst, SSM_SUPER), lambda b, k, c: (k, 0, 0)),
                  pl.BlockSpec((1, nst, SSM_SUPER), lambda b, k, c: (k, 0, 0)),
                  pw, pw,
                  pl.BlockSpec((1, SSM_SUPER), lambda b, k, c: (0, k))],
        out_specs=[blk, st, st],
        out_shape=[jax.ShapeDtypeStruct(u.shape, F32),
                   jax.ShapeDtypeStruct((bsz, 1, n_super * nst), F32),
                   jax.ShapeDtypeStruct((bsz, 1, n_super * nst), F32)],
        scratch_shapes=[pltpu.VMEM((SSM_SUPER // LANES, chunk, LANES), F32), pltpu.VMEM((chunk, 2 * nst), F32),
                        pltpu.VMEM((chunk, 2 * nst), BF16), pltpu.VMEM((2, nst), F32)],
        compiler_params=_params("parallel", "parallel", "arbitrary"),
        name="ssm_prompt",
    )(u, prm['wb'], prm['wc_re'], prm['wc_im'], prm['p_re'], prm['p_im'], d)
    return y, sre[:, 0], sim[:, 0]


def _ssm_step_kernel(u_ref, x0r_ref, x0i_ref, wb_ref, wcr_ref, wci_ref, pr_ref, pi_ref, d_ref,
                     y_ref, sre_ref, sim_ref):
    nst = x0r_ref.shape[1]
    u = u_ref[...]
    bu = jnp.dot(u.astype(BF16), wb_ref[0], preferred_element_type=F32)
    a_re, a_im = pr_ref[0:1, :], pi_ref[0:1, :]
    x0r, x0i = x0r_ref[...], x0i_ref[...]
    xr = a_re * x0r - a_im * x0i + bu[:, :nst]
    xi = a_re * x0i + a_im * x0r + bu[:, nst:]
    y = (jnp.dot(xr.astype(BF16), wcr_ref[0], preferred_element_type=F32)
         - jnp.dot(xi.astype(BF16), wci_ref[0], preferred_element_type=F32)
         + d_ref[...] * u)
    y_ref[...] = _gelu_tanh(y)
    sre_ref[...] = xr
    sim_ref[...] = xi


def ssm_step(u, x0_re, x0_im, prm, d):
    bd, width = u.shape
    n_super = width // SSM_SUPER
    nst = x0_re.shape[1] // n_super
    ub = pl.BlockSpec((bd, SSM_SUPER), lambda k: (0, k))
    st = pl.BlockSpec((bd, nst), lambda k: (0, k))
    pw = pl.BlockSpec((SUBLANES, nst), lambda k: (0, k))
    return pl.pallas_call(
        _ssm_step_kernel,
        grid=(n_super,),
        in_specs=[ub, st, st,
                  pl.BlockSpec((1, SSM_SUPER, 2 * nst), lambda k: (k, 0, 0)),
                  pl.BlockSpec((1, nst, SSM_SUPER), lambda k: (k, 0, 0)),
                  pl.BlockSpec((1, nst, SSM_SUPER), lambda k: (k, 0, 0)),
                  pw, pw,
                  pl.BlockSpec((1, SSM_SUPER), lambda k: (0, k))],
        out_specs=[ub, st, st],
        out_shape=[jax.ShapeDtypeStruct(u.shape, F32), jax.ShapeDtypeStruct(x0_re.shape, F32),
                   jax.ShapeDtypeStruct(x0_im.shape, F32)],
        compiler_params=_params("parallel"),
        name="ssm_step",
    )(u, x0_re, x0_im, prm['wb'], prm['wc_re'], prm['wc_im'], prm['p_re'], prm['p_im'], d)


def _run_layer(x, mod, wts, attend, ssm_fn):
    g, t, d = x.shape
    m = g * t
    md = lambda sub, kind: mod[:, :, sub * 3 + kind, :]
    flat = lambda a: a.reshape(m, a.shape[-1])
    gpre, gpost = wts['norm_pre'], wts['norm_post']
    att_w = wts['w_branch_attn'].shape[0]
    ssm_w = wts['w_branch_ssm'].shape[0]

    def ffn(h, gate_w, up_w, down_w):
        act = matmul_swiglu(flat(h), gate_w, up_w, down_w.shape[0])
        return matmul_kgrid(act, down_w, tk=down_w.shape[0] // 4).reshape(g, t, d)

    h = modulate(x, gpre[0:1], md(0, 0), md(0, 1))
    y = ffn(h, wts['ffn1_gate'], wts['ffn1_up'], wts['ffn1_down'])
    x, h = resid_update(x, y, gpost[0:1], md(0, 2), 0.5, (gpre[1:2], md(1, 0), md(1, 1)))

    hf = flat(h)
    w_in = wts['w_in']
    q, k32, k16, v32, v16, u, g_att, g_ssm = matmul_segments(hf, w_in, [
        (att_w, attend.q_scale, [attend.q_dtype]), (att_w, 1.0, [F32, BF16]), (att_w, 1.0, [F32, BF16]),
        (ssm_w, 1.0, [F32]), (d, 1.0, [BF16]), (d, 1.0, [BF16])])

    o = attend(q, k32, k16, v32, v16)
    y_s, s_re, s_im = ssm_fn(u)
    y_s = matmul_glu(flat(y_s), wts['ssm_w_glu'])
    merged = matmul_merge(o, wts['w_branch_attn'], g_att, y_s, wts['w_branch_ssm'], g_ssm)
    y = matmul_ws(merged, wts['w_out'], F32, name="w_out")
    x, h = resid_update(x, y.reshape(g, t, d), gpost[1:2], md(1, 2), 1.0, (gpre[2:3], md(2, 0), md(2, 1)))

    y = ffn(h, wts['ffn2_gate'], wts['ffn2_up'], wts['ffn2_down'])
    x, _ = resid_update(x, y, gpost[2:3], md(2, 2), 0.5)
    return x, k32, v32, s_re, s_im


class _Attend:
    def __init__(self, fn, q_scale, q_dtype):
        self.fn, self.q_scale, self.q_dtype = fn, q_scale, q_dtype

    def __call__(self, *a):
        return self.fn(*a)


def kernel(x_prompt, x_sample, cache_k, cache_v, state_ssm_re, state_ssm_im, page_table, c_prompt, c_sample, w_mod, b_mod, norm_pre, norm_post, ffn1_gate, ffn1_up, ffn1_down, w_in, lam_q, lam_k, attn_subln, w_branch_attn, ssm_lam_re, ssm_lam_im, ssm_log_dt, ssm_b_re, ssm_b_im, ssm_c_re, ssm_c_im, ssm_d, ssm_w_glu, w_branch_ssm, w_out, ffn2_gate, ffn2_up, ffn2_down):
    depth = w_mod.shape[0]
    bsz, s_len, d = x_prompt.shape
    bd, dec_seq, _ = x_sample.shape
    assert dec_seq == 1, "the decode attention handles one new token per sequence"
    heads = cache_k.shape[3]
    d_ff = ffn1_gate.shape[2]
    d_ff_pad = _round_up(d_ff, 1024)
    groups, nst = ssm_lam_re.shape[1:]

    xp, xs = x_prompt, x_sample.reshape(1, bd, d)
    outs = [[] for _ in range(8)]
    for li in range(depth):
        lam_init = 0.8 - 0.6 * math.exp(-0.3 * li)
        wts = dict(norm_pre=norm_pre[li], norm_post=norm_post[li],
                   ffn1_gate=ffn1_gate[li], ffn1_up=ffn1_up[li], ffn1_down=cast_pad_rows(ffn1_down[li], d_ff_pad),
                   ffn2_gate=ffn2_gate[li], ffn2_up=ffn2_up[li], ffn2_down=cast_pad_rows(ffn2_down[li], d_ff_pad),
                   w_in=w_in[li].astype(BF16), w_branch_attn=w_branch_attn[li], w_branch_ssm=w_branch_ssm[li],
                   ssm_w_glu=ssm_w_glu[li].astype(BF16), w_out=w_out[li])
        ssm_prm = ssm_prepare(ssm_lam_re[li], ssm_lam_im[li], ssm_log_dt[li], ssm_b_re[li], ssm_b_im[li],
                              ssm_c_re[li], ssm_c_im[li])
        d_row = ssm_d[li].reshape(1, -1)
        subln = attn_subln[li].reshape(1, -1)

        n_cond = bsz + bd
        c_all = jnp.pad(jnp.concatenate([c_prompt, c_sample], axis=0), ((0, _round_up(n_cond, 16) - n_cond), (0, 0)))
        mod = matmul_mod(c_all, w_mod[li], b_mod[li].reshape(1, -1))
        mod_p = mod[:bsz].reshape(bsz, 1, N_SUB * 3, d)
        mod_s = mod[bsz:n_cond].reshape(1, bd, N_SUB * 3, d)

        def attend_p(q, k32, k16, v32, v16):
            return attn_prompt(q, k16, v16, lam_q[li], lam_k[li], subln, bsz, lam_init)

        def attend_s(q, k32, k16, v32, v16):
            return attn_decode(q, k32, v32, cache_k[li], cache_v[li], page_table, lam_q[li], lam_k[li],
                               subln, lam_init)

        def ssm_p(u):
            y, s_re, s_im = ssm_prompt(u.reshape(bsz, s_len, -1), ssm_prm, d_row)
            return y, s_re, s_im

        def ssm_s(u):
            return ssm_step(u, state_ssm_re[li].reshape(bd, -1), state_ssm_im[li].reshape(bd, -1), ssm_prm, d_row)

        xp, k1, v1, r1, i1 = _run_layer(xp, mod_p, wts, _Attend(attend_p, QK_DIM ** -0.5 * LOG2_E, BF16), ssm_p)
        xs, k2, v2, r2, i2 = _run_layer(xs, mod_s, wts, _Attend(attend_s, QK_DIM ** -0.5, F32), ssm_s)
        new = [k1.reshape(bsz, s_len, heads, HEAD_COLS), v1.reshape(bsz, s_len, heads, HEAD_COLS),
               r1.reshape(bsz, groups, nst), i1.reshape(bsz, groups, nst),
               k2.reshape(bd, 1, heads, HEAD_COLS), v2.reshape(bd, 1, heads, HEAD_COLS),
               r2.reshape(bd, groups, nst), i2.reshape(bd, groups, nst)]
        for acc, val in zip(outs, new):
            acc.append(val)
    return (xp, xs.reshape(bd, 1, d), *[jnp.stack(o) for o in outs])
```

```python
import functools
import math

import numpy as np
import jax
import jax.numpy as jnp
from jax import lax
from jax.experimental import pallas as pl
from jax.experimental.pallas import tpu as pltpu

F32 = jnp.float32
BF16 = jnp.bfloat16
EPS = 1e-6
NEG = -1e30

LOG2_E = math.log2(math.e)
QK_DIM = 128
HEAD_COLS = 2 * QK_DIM
SSM_GROUP = 16
SSM_STATE = 64
N_SUB = 3
SUBLANES = 8
LANES = 128
SSM_SUPER = 256
SSM_SEG = 64
VMEM_LIMIT_BYTES = 56 * 1024 * 1024


def _params(*sem):
    return pltpu.CompilerParams(dimension_semantics=sem, vmem_limit_bytes=VMEM_LIMIT_BYTES)


def _round_up(x, m):
    return (x + m - 1) // m * m


def _sigmoid(x):
    return 1.0 / (1.0 + jnp.exp(-x))


def _rms(x, gain):
    return x * lax.rsqrt(jnp.mean(x * x, axis=-1, keepdims=True) + EPS) * gain


def _mm_segments_kernel(x_ref, xs_ref, w_ref, *o_refs, segs):
    i, j = pl.program_id(0), pl.program_id(1)
    pos = 0
    for start, count, (scale_m, dtypes_m), (scale_s, dtypes_s) in segs:
        outs_m = o_refs[pos:pos + len(dtypes_m)]
        outs_s = o_refs[pos + len(dtypes_m):pos + len(dtypes_m) + len(dtypes_s)]
        pos += len(dtypes_m) + len(dtypes_s)
        in_seg = jnp.logical_and(j >= start, j < start + count)

        def emit(src_ref, outs, scale):
            acc = jnp.dot(src_ref[...], w_ref[...], preferred_element_type=F32)
            if scale != 1.0:
                acc = acc * scale
            for o_ref in outs:
                o_ref[...] = acc.astype(o_ref.dtype)

        pl.when(in_seg)(functools.partial(emit, x_ref, outs_m, scale_m))
        pl.when(jnp.logical_and(in_seg, i == 0))(functools.partial(emit, xs_ref, outs_s, scale_s))


def matmul_segments(x, x_side, w, segments, *, tm=1024, tn=512):
    m, k = x.shape
    ms = x_side.shape[0]
    tm = min(tm, m)
    segs, in_tiles, out_specs, out_shape, is_side = [], 0, [], [], []
    for n_cols, fmt_m, fmt_s in segments:
        start, count = in_tiles, n_cols // tn
        segs.append((start, count, (fmt_m[0], tuple(fmt_m[1])), (fmt_s[0], tuple(fmt_s[1]))))
        in_tiles += count
        for dt in fmt_m[1]:
            out_specs.append(pl.BlockSpec(
                (tm, tn), lambda i, j, start=start, count=count: (i, jnp.clip(j - start, 0, count - 1))))
            out_shape.append(jax.ShapeDtypeStruct((m, n_cols), dt))
            is_side.append(False)
        for dt in fmt_s[1]:
            out_specs.append(pl.BlockSpec(
                (ms, tn), lambda i, j, start=start, count=count:
                (0, jnp.where(i == 0, jnp.clip(j - start, 0, count - 1), count - 1))))
            out_shape.append(jax.ShapeDtypeStruct((ms, n_cols), dt))
            is_side.append(True)
    assert in_tiles * tn == w.shape[1]
    outs = pl.pallas_call(
        functools.partial(_mm_segments_kernel, segs=tuple(segs)),
        grid=(m // tm, in_tiles),
        in_specs=[pl.BlockSpec((tm, k), lambda i, j: (i, 0)),
                  pl.BlockSpec((ms, k), lambda i, j: (0, 0)),
                  pl.BlockSpec((k, tn), lambda i, j: (0, j))],
        out_specs=out_specs, out_shape=out_shape,
        compiler_params=_params("arbitrary", "arbitrary"),
        name="w_in_segments",
    )(x, x_side, w)
    return ([o for o, s in zip(outs, is_side) if not s], [o for o, s in zip(outs, is_side) if s])


def _mm_ws_kernel(x_ref, xs_ref, w_ref, o_ref, os_ref, w_sc):
    first = pl.program_id(1) == 0
    _cast_weights_once([(w_ref, w_sc)], first)
    o_ref[...] = jnp.dot(x_ref[...], w_sc[...], preferred_element_type=F32).astype(o_ref.dtype)

    @pl.when(first)
    def _():
        os_ref[...] = jnp.dot(xs_ref[...], w_sc[...], preferred_element_type=F32).astype(os_ref.dtype)


def matmul_ws(x, x_side, w, out_dtype, *, tm=1024, tn=512, name):
    m, k = x.shape
    ms = x_side.shape[0]
    n = w.shape[1]
    tm, tn = min(tm, m), min(tn, n)
    return pl.pallas_call(
        _mm_ws_kernel,
        grid=(n // tn, m // tm),
        in_specs=[pl.BlockSpec((tm, k), lambda j, i: (i, 0)),
                  pl.BlockSpec((ms, k), lambda j, i: (0, 0)),
                  pl.BlockSpec((k, tn), lambda j, i: (0, j))],
        out_specs=[pl.BlockSpec((tm, tn), lambda j, i: (i, j)), pl.BlockSpec((ms, tn), lambda j, i: (0, j))],
        out_shape=[jax.ShapeDtypeStruct((m, n), out_dtype), jax.ShapeDtypeStruct((ms, n), out_dtype)],
        scratch_shapes=[pltpu.VMEM((k, tn), BF16)],
        compiler_params=_params("arbitrary", "arbitrary"),
        name=name,
    )(x, x_side, w)


def _cast_pad_rows_kernel(w_ref, o_ref, *, n_rows):
    tr = w_ref.shape[0]
    row = lax.broadcasted_iota(jnp.int32, w_ref.shape, 0) + pl.program_id(0) * tr
    o_ref[...] = jnp.where(row < n_rows, w_ref[...], 0.0).astype(o_ref.dtype)


def cast_pad_rows(w, n_rows_out, *, tr=512):
    r, c = w.shape
    return pl.pallas_call(
        functools.partial(_cast_pad_rows_kernel, n_rows=r),
        grid=(n_rows_out // tr,),
        in_specs=[pl.BlockSpec((tr, c), lambda i: (jnp.minimum(i, pl.cdiv(r, tr) - 1), 0))],
        out_specs=pl.BlockSpec((tr, c), lambda i: (i, 0)),
        out_shape=jax.ShapeDtypeStruct((n_rows_out, c), BF16),
        compiler_params=_params("parallel"),
        name="cast_pad_rows",
    )(w)


def _mm_kgrid_kernel(x_ref, xs_ref, w_ref, o_ref, os_ref):
    kk = pl.program_id(2)

    def accumulate(src_ref, dst_ref):
        part = jnp.dot(src_ref[...], w_ref[...], preferred_element_type=F32)

        @pl.when(kk == 0)
        def _():
            dst_ref[...] = part

        @pl.when(kk != 0)
        def _():
            dst_ref[...] += part

    accumulate(x_ref, o_ref)
    pl.when(pl.program_id(0) == 0)(functools.partial(accumulate, xs_ref, os_ref))


def matmul_kgrid(x, x_side, w, *, tm=1024, tn=1024, tk):
    m, k = x.shape
    ms = x_side.shape[0]
    n = w.shape[1]
    tm, tn = min(tm, m), min(tn, n)
    nj, nk = n // tn, k // tk
    return pl.pallas_call(
        _mm_kgrid_kernel,
        grid=(m // tm, nj, nk),
        in_specs=[pl.BlockSpec((tm, tk), lambda i, j, kk: (i, kk)),
                  pl.BlockSpec((ms, tk), lambda i, j, kk: (0, jnp.where(i == 0, kk, nk - 1))),
                  pl.BlockSpec((tk, tn), lambda i, j, kk: (kk, j))],
        out_specs=[pl.BlockSpec((tm, tn), lambda i, j, kk: (i, j)),
                   pl.BlockSpec((ms, tn), lambda i, j, kk: (0, jnp.where(i == 0, j, nj - 1)))],
        out_shape=[jax.ShapeDtypeStruct((m, n), F32), jax.ShapeDtypeStruct((ms, n), F32)],
        compiler_params=_params("arbitrary", "arbitrary", "arbitrary"),
        name="ffn_down",
    )(x, x_side, w)


def _cast_weights_once(pairs, do_cast):
    @pl.when(do_cast)
    def _():
        for src, dst in pairs:
            dst[...] = src[...].astype(BF16)


def _mm_swiglu_kernel(x_ref, xs_ref, wg_ref, wu_ref, o_ref, os_ref, wg_sc, wu_sc, *, n_real):
    j, i = pl.program_id(0), pl.program_id(1)
    real = j < n_real
    _cast_weights_once([(wg_ref, wg_sc), (wu_ref, wu_sc)], jnp.logical_and(i == 0, real))

    def emit(src_ref, dst_ref):
        x = src_ref[...]
        g = jnp.dot(x, wg_sc[...], preferred_element_type=F32)
        u = jnp.dot(x, wu_sc[...], preferred_element_type=F32)
        dst_ref[...] = (g * _sigmoid(g) * u).astype(dst_ref.dtype)

    def emit_zeros(dst_ref):
        dst_ref[...] = jnp.zeros(dst_ref.shape, dst_ref.dtype)

    pl.when(real)(functools.partial(emit, x_ref, o_ref))
    pl.when(jnp.logical_not(real))(functools.partial(emit_zeros, o_ref))
    pl.when(jnp.logical_and(i == 0, real))(functools.partial(emit, xs_ref, os_ref))
    pl.when(jnp.logical_and(i == 0, jnp.logical_not(real)))(functools.partial(emit_zeros, os_ref))


def matmul_swiglu(x, x_side, wg, wu, n_out, *, tm=1024, tn=256):
    m, k = x.shape
    ms = x_side.shape[0]
    n = wg.shape[1]
    tm = min(tm, m)
    n_real = n // tn
    w_spec = pl.BlockSpec((k, tn), lambda j, i: (0, jnp.minimum(j, n_real - 1)))
    return pl.pallas_call(
        functools.partial(_mm_swiglu_kernel, n_real=n_real),
        grid=(n_out // tn, m // tm),
        in_specs=[pl.BlockSpec((tm, k), lambda j, i: (jnp.where(j < n_real, i, 0), 0)),
                  pl.BlockSpec((ms, k), lambda j, i: (0, 0)), w_spec, w_spec],
        out_specs=[pl.BlockSpec((tm, tn), lambda j, i: (i, j)), pl.BlockSpec((ms, tn), lambda j, i: (0, j))],
        out_shape=[jax.ShapeDtypeStruct((m, n_out), BF16), jax.ShapeDtypeStruct((ms, n_out), BF16)],
        scratch_shapes=[pltpu.VMEM((k, tn), BF16), pltpu.VMEM((k, tn), BF16)],
        compiler_params=_params("arbitrary", "arbitrary"),
        name="ffn_up_swiglu",
    )(x, x_side, wg, wu)


def _mm_glu_kernel(x_ref, w_ref, e_ref, o_ref):
    acc = jnp.dot(x_ref[...].astype(BF16), w_ref[...], preferred_element_type=F32)
    o_ref[...] = (e_ref[...] * _sigmoid(acc)).astype(o_ref.dtype)


def matmul_glu(y, w, *, tm=1024, tn=512):
    m, k = y.shape
    tm, tn = min(tm, m), min(tn, k)
    return pl.pallas_call(
        _mm_glu_kernel,
        grid=(m // tm, k // tn),
        in_specs=[pl.BlockSpec((tm, k), lambda i, j: (i, 0)),
                  pl.BlockSpec((k, tn), lambda i, j: (0, j)),
                  pl.BlockSpec((tm, tn), lambda i, j: (i, j))],
        out_specs=pl.BlockSpec((tm, tn), lambda i, j: (i, j)),
        out_shape=jax.ShapeDtypeStruct((m, k), BF16),
        compiler_params=_params("parallel", "arbitrary"),
        name="ssm_glu",
    )(y, w, y)


def _mm_merge_kernel(xa_ref, ga_ref, xs_ref, gs_ref, xa2_ref, ga2_ref, xs2_ref, gs2_ref, wa_ref, ws_ref,
                     o_ref, o2_ref, wa_sc, ws_sc):
    first = pl.program_id(1) == 0
    _cast_weights_once([(wa_ref, wa_sc), (ws_ref, ws_sc)], first)

    def emit(xa, ga, xs, gs, dst_ref):
        ya = jnp.dot(xa[...], wa_sc[...], preferred_element_type=F32)
        ys = jnp.dot(xs[...], ws_sc[...], preferred_element_type=F32)
        merged = _sigmoid(ga[...].astype(F32)) * ya + _sigmoid(gs[...].astype(F32)) * ys
        dst_ref[...] = merged.astype(dst_ref.dtype)

    emit(xa_ref, ga_ref, xs_ref, gs_ref, o_ref)
    pl.when(first)(functools.partial(emit, xa2_ref, ga2_ref, xs2_ref, gs2_ref, o2_ref))


def matmul_merge(main, side, wa, ws, *, tm=1024, tn=512):
    m, ka = main[0].shape
    ms = side[0].shape[0]
    ks = main[2].shape[1]
    n = wa.shape[1]
    tm, tn = min(tm, m), min(tn, n)
    tile = pl.BlockSpec((tm, tn), lambda j, i: (i, j))
    tile2 = pl.BlockSpec((ms, tn), lambda j, i: (0, j))
    rows = lambda kdim: pl.BlockSpec((tm, kdim), lambda j, i: (i, 0))
    rows2 = lambda kdim: pl.BlockSpec((ms, kdim), lambda j, i: (0, 0))
    return pl.pallas_call(
        _mm_merge_kernel,
        grid=(n // tn, m // tm),
        in_specs=[rows(ka), tile, rows(ks), tile, rows2(ka), tile2, rows2(ks), tile2,
                  pl.BlockSpec((ka, tn), lambda j, i: (0, j)), pl.BlockSpec((ks, tn), lambda j, i: (0, j))],
        out_specs=[tile, tile2],
        out_shape=[jax.ShapeDtypeStruct((m, n), BF16), jax.ShapeDtypeStruct((ms, n), BF16)],
        scratch_shapes=[pltpu.VMEM((ka, tn), BF16), pltpu.VMEM((ks, tn), BF16)],
        compiler_params=_params("arbitrary", "arbitrary"),
        name="branch_merge",
    )(*main, *side, wa, ws)


def _mm_mod_kernel(c_ref, w_ref, b_ref, o_ref):
    c = c_ref[...]
    lhs = (c * _sigmoid(c)).astype(BF16)
    acc = jnp.dot(lhs, w_ref[...].astype(BF16), preferred_element_type=F32)
    o_ref[...] = acc + b_ref[...]


def matmul_mod(c, w, b, *, tn=512):
    m, k = c.shape
    n = w.shape[1]
    return pl.pallas_call(
        _mm_mod_kernel,
        grid=(n // tn,),
        in_specs=[pl.BlockSpec((m, k), lambda j: (0, 0)),
                  pl.BlockSpec((k, tn), lambda j: (0, j)),
                  pl.BlockSpec((1, tn), lambda j: (0, j))],
        out_specs=pl.BlockSpec((m, tn), lambda j: (0, j)),
        out_shape=jax.ShapeDtypeStruct((m, n), F32),
        compiler_params=_params("arbitrary"),
        name="adaln_mod",
    )(c, w, b)


def _modulate_kernel(x_ref, g_ref, sh_ref, sc_ref, h_ref):
    h = _rms(x_ref[0], g_ref[...]) * (1.0 + sc_ref[0]) + sh_ref[0]
    h_ref[0] = h.astype(h_ref.dtype)


def _resid_kernel(x_ref, y_ref, gpost_ref, gate_ref, *rest, coef, with_next):
    xn = x_ref[0] + coef * gate_ref[0] * _rms(y_ref[0], gpost_ref[...])
    if with_next:
        gpre_ref, sh_ref, sc_ref, xo_ref, h_ref = rest
        h = _rms(xn, gpre_ref[...]) * (1.0 + sc_ref[0]) + sh_ref[0]
        h_ref[0] = h.astype(h_ref.dtype)
    else:
        (xo_ref,) = rest
    xo_ref[0] = xn


def _row_specs(x, mod_rows, tr):
    g, t, d = x.shape
    tr = min(tr, t)
    row = pl.BlockSpec((1, tr, d), lambda gi, ti: (gi, ti, 0))
    gain = pl.BlockSpec((1, d), lambda gi, ti: (0, 0))
    if mod_rows == 1:
        mod = pl.BlockSpec((1, 1, d), lambda gi, ti: (gi, 0, 0))
    else:
        mod = pl.BlockSpec((1, tr, d), lambda gi, ti: (gi, ti, 0))
    return (g, t // tr), row, gain, mod


def modulate(x, gain, shift, scale, *, tr=256):
    grid, row, gspec, mod = _row_specs(x, shift.shape[1], tr)
    return pl.pallas_call(
        _modulate_kernel, grid=grid,
        in_specs=[row, gspec, mod, mod], out_specs=row,
        out_shape=jax.ShapeDtypeStruct(x.shape, BF16),
        compiler_params=_params("parallel", "parallel"),
        name="modulate",
    )(x, gain, shift, scale)


def resid_update(x, y, gpost, gate, coef, nxt=None, *, tr=256):
    grid, row, gspec, mod = _row_specs(x, gate.shape[1], tr)
    in_specs = [row, row, gspec, mod]
    args = [x, y, gpost, gate]
    out_specs = [row]
    out_shape = [jax.ShapeDtypeStruct(x.shape, F32)]
    if nxt is not None:
        in_specs += [gspec, mod, mod]
        args += list(nxt)
        out_specs.append(row)
        out_shape.append(jax.ShapeDtypeStruct(x.shape, BF16))
    outs = pl.pallas_call(
        functools.partial(_resid_kernel, coef=coef, with_next=nxt is not None), grid=grid,
        in_specs=in_specs, out_specs=out_specs, out_shape=out_shape,
        compiler_params=_params("parallel", "parallel"),
        name="resid_update",
    )(*args)
    return outs if nxt is not None else (outs[0], None)


def _diff_lambda(lamq_ref, lamk_ref, lam_init):
    prod = lamq_ref[...] * lamk_ref[...]
    s0 = jnp.sum(prod[0:1], axis=-1, keepdims=True)
    s1 = jnp.sum(prod[1:2], axis=-1, keepdims=True)
    return jnp.exp(s0) - jnp.exp(s1) + lam_init


def _decode_pages(qm_bf, k_refs, v_refs, state):
    m_prev, l_prev, acc_prev = state
    page, heads, _ = k_refs[0].shape[1:]
    n_rows, n_keys = 2 * heads, page * heads
    row = lax.broadcasted_iota(jnp.int32, (n_rows, n_keys), 0)
    col = lax.broadcasted_iota(jnp.int32, (n_rows, n_keys), 1)
    same_head = (col & (heads - 1)) == (row & (heads - 1))
    scores = []
    for k_ref in k_refs:
        k2 = k_ref[0].reshape(n_keys, HEAD_COLS).astype(BF16)
        s = lax.dot_general(qm_bf, k2, (((1,), (1,)), ((), ())), preferred_element_type=F32)
        scores.append(jnp.where(same_head, s, NEG))
    m_new = m_prev
    for s in scores:
        m_new = jnp.maximum(m_new, jnp.max(s, axis=-1, keepdims=True))
    corr = jnp.exp(m_prev - m_new)
    l_new = l_prev * corr
    acc = acc_prev * corr
    for s, v_ref in zip(scores, v_refs):
        p = jnp.exp(s - m_new)
        l_new = l_new + jnp.sum(p, axis=-1, keepdims=True)
        v2 = v_ref[0].reshape(n_keys, HEAD_COLS).astype(BF16)
        acc = acc + jnp.dot(p.astype(BF16), v2, preferred_element_type=F32)
    return m_new, l_new, acc


def _attn_kernel(qi_tab, ki_tab, pt_ref, lamq_ref, lamk_ref, subln_ref, q_ref, k_ref, v_ref,
                 qm_ref, kn_ref, vn_ref, *rest, n_pp, lam_init, steps_per_seq, n_dec_steps):
    k_refs, v_refs = rest[:n_pp], rest[n_pp:2 * n_pp]
    o_ref, od_ref, m_sc, l_sc, acc_sc, dm_sc, dl_sc, dacc_sc = rest[2 * n_pp:]
    t = pl.program_id(2)
    qi, ki = qi_tab[t], ki_tab[t]
    tq, tk = q_ref.shape[0], k_ref.shape[0]
    heads = kn_ref.shape[1]
    step = (pl.program_id(0) * pl.num_programs(1) + pl.program_id(1)) * pl.num_programs(2) + t
    page_group = lax.rem(step, steps_per_seq)
    qm = qm_ref[0]

    @pl.when(ki == 0)
    def _():
        m_sc[...] = jnp.full(m_sc.shape, NEG, F32)
        l_sc[...] = jnp.zeros(l_sc.shape, F32)
        acc_sc[...] = jnp.zeros(acc_sc.shape, F32)

    def advance(diagonal):
        q, k, v = q_ref[...], k_ref[...], v_ref[...]
        for mp in range(2):
            cols = slice(mp * QK_DIM, (mp + 1) * QK_DIM)
            s = lax.dot_general(q[:, cols], k[:, cols], (((1,), (1,)), ((), ())),
                                preferred_element_type=F32)
            if diagonal:
                row = lax.broadcasted_iota(jnp.int32, (tq, tk), 0)
                col = lax.broadcasted_iota(jnp.int32, (tq, tk), 1)
                s = jnp.where(col <= row, s, NEG)
            m_prev = m_sc[mp]
            m_new = jnp.maximum(m_prev, jnp.max(s, axis=-1, keepdims=True))
            corr = jnp.exp2(m_prev - m_new)
            p = jnp.exp2(s - m_new)
            l_sc[mp] = l_sc[mp] * corr + jnp.sum(p, axis=-1, keepdims=True)
            acc_sc[mp] = acc_sc[mp] * corr + jnp.dot(p.astype(BF16), v, preferred_element_type=F32)
            m_sc[mp] = m_new
        fresh = page_group == 0
        state = (jnp.where(fresh, NEG, dm_sc[...]), jnp.where(fresh, 0.0, dl_sc[...]),
                 jnp.where(fresh, 0.0, dacc_sc[...]))
        dm_sc[...], dl_sc[...], dacc_sc[...] = _decode_pages(qm.astype(BF16), k_refs, v_refs, state)

    @pl.when(ki < qi)
    def _():
        advance(False)

    @pl.when(ki == qi)
    def _():
        advance(True)
        lam = _diff_lambda(lamq_ref, lamk_ref, lam_init)
        o = acc_sc[0] / l_sc[0] - lam * (acc_sc[1] / l_sc[1])
        o_ref[...] = (_rms(o, subln_ref[...]) * (1.0 - lam_init)).astype(o_ref.dtype)

    @pl.when(jnp.logical_and(step < n_dec_steps, page_group == steps_per_seq - 1))
    def _():
        k_new = jnp.concatenate([kn_ref[0], kn_ref[0]], axis=0)
        v_new = jnp.concatenate([vn_ref[0], vn_ref[0]], axis=0)
        s_new = jnp.sum(qm * k_new, axis=-1, keepdims=True)
        m_prev = dm_sc[...]
        m_new = jnp.maximum(m_prev, s_new)
        corr = jnp.exp(m_prev - m_new)
        p_new = jnp.exp(s_new - m_new)
        w = (dacc_sc[...] * corr + p_new * v_new) / (dl_sc[...] * corr + p_new)
        lam = _diff_lambda(lamq_ref, lamk_ref, lam_init)
        o = w[:heads] - lam * w[heads:]
        od_ref[0] = (_rms(o, subln_ref[...]) * (1.0 - lam_init)).astype(od_ref.dtype)


def attention(q, k, v, qd, kd_new, vd_new, cache_k, cache_v, page_table, lam_q, lam_k, subln, bsz, lam_init,
              *, tq=512, min_pages_per_step=4):
    m, width = q.shape
    s_len = m // bsz
    heads = width // HEAD_COLS
    assert heads == SUBLANES, "cache pages are viewed as (page*heads, 256) row tiles"
    tq = min(tq, s_len)
    nq = s_len // tq
    pairs = [(a, b) for a in range(nq) for b in range(a + 1)]
    qi_tab = jnp.asarray(np.array([p[0] for p in pairs], np.int32))
    ki_tab = jnp.asarray(np.array([p[1] for p in pairs], np.int32))
    n_steps = bsz * heads * len(pairs)

    bd, n_pages = page_table.shape
    page = cache_k.shape[1]
    n_pp = next(c for c in range(min_pages_per_step, n_pages + 1)
                if n_pages % c == 0 and bd * (n_pages // c) <= n_steps)
    steps_per_seq = n_pages // n_pp
    n_dec_steps = bd * steps_per_seq
    q4 = qd.reshape(bd, heads, 2, QK_DIM).transpose(0, 2, 1, 3)
    qm = jnp.einsum('bmhd,mn->bmhnd', q4, jnp.eye(2, dtype=F32)).reshape(bd, 2 * heads, HEAD_COLS)

    def dec_step(b, h, t):
        return jnp.minimum((b * heads + h) * len(pairs) + t, n_dec_steps - 1)

    small = lambda shape: pl.BlockSpec(shape, lambda b, h, t, qt, kt, pt: (0, 0))
    tok = lambda rows: pl.BlockSpec(
        (1, rows, HEAD_COLS), lambda b, h, t, qt, kt, pt: (dec_step(b, h, t) // steps_per_seq, 0, 0))

    def page_spec(j):
        def index(b, h, t, qt, kt, pt):
            n = dec_step(b, h, t)
            return (pt[n // steps_per_seq, lax.rem(n, steps_per_seq) * n_pp + j], 0, 0, 0)
        return pl.BlockSpec((1, page, heads, HEAD_COLS), index)

    q_blk = pl.BlockSpec((tq, HEAD_COLS), lambda b, h, t, qt, kt, pt: (b * nq + qt[t], h))
    kv_blk = pl.BlockSpec((tq, HEAD_COLS), lambda b, h, t, qt, kt, pt: (b * nq + kt[t], h))
    grid_spec = pltpu.PrefetchScalarGridSpec(
        num_scalar_prefetch=3,
        grid=(bsz, heads, len(pairs)),
        in_specs=[small(lam_q.shape), small(lam_k.shape), small(subln.shape), q_blk, kv_blk, kv_blk,
                  tok(2 * heads), tok(heads), tok(heads)] + [page_spec(j) for j in range(n_pp)] * 2,
        out_specs=[q_blk, tok(heads)],
        scratch_shapes=[pltpu.VMEM((2, tq, 1), F32), pltpu.VMEM((2, tq, 1), F32),
                        pltpu.VMEM((2, tq, HEAD_COLS), F32),
                        pltpu.VMEM((2 * heads, 1), F32), pltpu.VMEM((2 * heads, 1), F32),
                        pltpu.VMEM((2 * heads, HEAD_COLS), F32)])
    o, od = pl.pallas_call(
        functools.partial(_attn_kernel, n_pp=n_pp, lam_init=lam_init, steps_per_seq=steps_per_seq,
                          n_dec_steps=n_dec_steps),
        grid_spec=grid_spec,
        out_shape=[jax.ShapeDtypeStruct((m, width), BF16), jax.ShapeDtypeStruct((bd, heads, HEAD_COLS), BF16)],
        compiler_params=_params("arbitrary", "arbitrary", "arbitrary"),
        name="attention",
    )(qi_tab, ki_tab, page_table, lam_q, lam_k, subln, q, k, v, qm,
      kd_new.reshape(bd, heads, HEAD_COLS), vd_new.reshape(bd, heads, HEAD_COLS),
      *([cache_k] * n_pp), *([cache_v] * n_pp))
    return o, od.reshape(bd, width)


def _ssm_input_kernel(lr_ref, li_ref, ldt_ref, br_ref, bi_ref, bbr_ref, bbi_ref):
    lr, li = lr_ref[...], li_ref[...]
    dt = jnp.exp(ldt_ref[...])
    mag = jnp.exp(dt * lr)
    a_re, a_im = mag * jnp.cos(dt * li), mag * jnp.sin(dt * li)
    den = lr * lr + li * li
    f_re = ((a_re - 1.0) * lr + a_im * li) / den
    f_im = (a_im * lr - (a_re - 1.0) * li) / den
    br, bi = br_ref[...], bi_ref[...]
    bbr_ref[...] = f_re * br - f_im * bi
    bbi_ref[...] = f_re * bi + f_im * br


def _ssm_power_kernel(lr_ref, li_ref, ldt_ref, pr_ref, pi_ref):
    steps = (lax.broadcasted_iota(jnp.int32, pr_ref.shape, 0) + 1).astype(F32)
    dt = jnp.exp(ldt_ref[...])
    mag = jnp.exp(steps * (dt * lr_ref[...]))
    ang = steps * (dt * li_ref[...])
    pr_ref[...] = mag * jnp.cos(ang)
    pi_ref[...] = mag * jnp.sin(ang)


def ssm_prepare(lam_re, lam_im, log_dt, b_re, b_im, c_re, c_im):
    groups, nst = lam_re.shape
    gps = SSM_SUPER // SSM_GROUP
    n_super = groups // gps
    rep = lambda a: jnp.repeat(a, SSM_GROUP, axis=0)
    ldt_gn = jnp.broadcast_to(log_dt[:, None], (groups, nst))
    full = lambda a: pl.BlockSpec(a.shape, lambda: (0,) * a.ndim)
    bt = lambda b: b.transpose(0, 2, 1).reshape(groups * SSM_GROUP, nst)
    ins = [rep(lam_re), rep(lam_im), rep(ldt_gn), bt(b_re), bt(b_im)]
    bbr, bbi = pl.pallas_call(
        _ssm_input_kernel,
        in_specs=[full(a) for a in ins], out_specs=[full(ins[0])] * 2,
        out_shape=[jax.ShapeDtypeStruct(ins[0].shape, F32)] * 2,
    )(*ins)
    flat = lambda a: a.reshape(1, groups * nst)
    ins = [flat(lam_re), flat(lam_im), flat(ldt_gn)]
    pw = jax.ShapeDtypeStruct((SSM_SEG, groups * nst), F32)
    p_re, p_im = pl.pallas_call(
        _ssm_power_kernel,
        in_specs=[full(a) for a in ins], out_specs=[pl.BlockSpec(pw.shape, lambda: (0, 0))] * 2,
        out_shape=[pw, pw],
    )(*ins)
    eye = jnp.eye(gps, dtype=F32)

    def in_blocks(bb):
        x = bb.reshape(n_super, gps, SSM_GROUP, nst)
        return jnp.einsum('kgsn,gh->kgshn', x, eye).reshape(n_super, SSM_SUPER, gps * nst)

    def out_blocks(c):
        x = c.reshape(n_super, gps, SSM_GROUP, nst)
        return jnp.einsum('kgsn,gh->kgnhs', x, eye).reshape(n_super, gps * nst, SSM_SUPER).astype(BF16)

    wb = jnp.concatenate([in_blocks(bbr), in_blocks(bbi)], axis=-1).astype(BF16)
    return dict(wb=wb, wc_re=out_blocks(c_re), wc_im=out_blocks(c_im), p_re=p_re, p_im=p_im)


def _gelu_tanh(x):
    return 0.5 * x * (1.0 + jnp.tanh(math.sqrt(2.0 / math.pi) * (x + 0.044715 * (x * x * x))))


def _ssm_prompt_kernel(u_ref, wb_ref, wcr_ref, wci_ref, pr_ref, pi_ref, d_ref, y_ref, sre_ref, sim_ref,
                       perm_sc, bu_sc, xb_sc, car_sc):
    c = pl.program_id(2)
    seg = pr_ref.shape[0]
    nst = pr_ref.shape[1]
    re, im = slice(0, nst), slice(nst, 2 * nst)

    @pl.when(c == 0)
    def _():
        car_sc[...] = jnp.zeros(car_sc.shape, F32)

    n_lane_tiles = perm_sc.shape[0]
    for s in range(SUBLANES):
        for lt in range(n_lane_tiles):
            perm_sc[lt, pl.ds(s, seg, stride=SUBLANES), :] = (
                u_ref[0, s * seg:(s + 1) * seg, lt * LANES:(lt + 1) * LANES])
    up = jnp.concatenate([perm_sc[lt] for lt in range(n_lane_tiles)], axis=-1)
    bu_sc[...] = jnp.dot(up.astype(BF16), wb_ref[0], preferred_element_type=F32)

    a_re = jnp.broadcast_to(pr_ref[0:1, :], (SUBLANES, nst))
    a_im = jnp.broadcast_to(pi_ref[0:1, :], (SUBLANES, nst))

    def scan_body(i, carry):
        xr, xi = carry
        rows = pl.ds(pl.multiple_of(i * SUBLANES, SUBLANES), SUBLANES)
        nr = a_re * xr - a_im * xi + bu_sc[rows, re]
        ni = a_re * xi + a_im * xr + bu_sc[rows, im]
        bu_sc[rows, re] = nr
        bu_sc[rows, im] = ni
        return nr, ni

    zero = jnp.zeros((SUBLANES, nst), F32)
    er, ei = lax.fori_loop(0, seg, scan_body, (zero, zero), unroll=4)

    s_re, s_im = pr_ref[seg - 1:seg, :], pi_ref[seg - 1:seg, :]
    hr, hi = car_sc[0:1, :], car_sc[1:2, :]
    starts_r, starts_i = [], []
    for s in range(SUBLANES):
        starts_r.append(hr)
        starts_i.append(hi)
        hr, hi = (s_re * hr - s_im * hi + er[s:s + 1], s_re * hi + s_im * hr + ei[s:s + 1])
    car_sc[0:1, :] = hr
    car_sc[1:2, :] = hi
    h0r = jnp.concatenate(starts_r, axis=0)
    h0i = jnp.concatenate(starts_i, axis=0)

    def fix_body(i2, _):
        halves_r, halves_i = [], []
        for half in range(2):
            i = i2 * 2 + half
            rows = pl.ds(pl.multiple_of(i * SUBLANES, SUBLANES), SUBLANES)
            pr, pi = pr_ref[pl.ds(i, 1), :], pi_ref[pl.ds(i, 1), :]
            halves_r.append(bu_sc[rows, re] + (pr * h0r - pi * h0i))
            halves_i.append(bu_sc[rows, im] + (pr * h0i + pi * h0r))
        rows16 = pl.ds(pl.multiple_of(i2 * 2 * SUBLANES, 2 * SUBLANES), 2 * SUBLANES)
        xb_sc[rows16, re] = jnp.concatenate(halves_r, axis=0).astype(BF16)
        xb_sc[rows16, im] = jnp.concatenate(halves_i, axis=0).astype(BF16)
        return 0

    lax.fori_loop(0, seg // 2, fix_body, 0, unroll=2)

    y = (jnp.dot(xb_sc[:, re], wcr_ref[0], preferred_element_type=F32)
         - jnp.dot(xb_sc[:, im], wci_ref[0], preferred_element_type=F32)
         + d_ref[...] * up)
    y = _gelu_tanh(y)
    for lt in range(n_lane_tiles):
        perm_sc[lt] = y[:, lt * LANES:(lt + 1) * LANES]
    for s in range(SUBLANES):
        for lt in range(n_lane_tiles):
            y_ref[0, s * seg:(s + 1) * seg, lt * LANES:(lt + 1) * LANES] = (
                perm_sc[lt, pl.ds(s, seg, stride=SUBLANES), :])

    @pl.when(c == pl.num_programs(2) - 1)
    def _():
        sre_ref[0] = hr
        sim_ref[0] = hi


def ssm_prompt(u, prm, d):
    bsz, s_len, width = u.shape
    n_super = width // SSM_SUPER
    nst = prm['p_re'].shape[1] // n_super
    chunk = SUBLANES * SSM_SEG
    blk = pl.BlockSpec((1, chunk, SSM_SUPER), lambda b, k, c: (b, c, k))
    st = pl.BlockSpec((1, 1, nst), lambda b, k, c: (b, 0, k))
    pw = pl.BlockSpec((SSM_SEG, nst), lambda b, k, c: (0, k))
    y, sre, sim = pl.pallas_call(
        _ssm_prompt_kernel,
        grid=(bsz, n_super, s_len // chunk),
        in_specs=[blk,
                  pl.BlockSpec((1, SSM_SUPER, 2 * nst), lambda b, k, c: (k, 0, 0)),
                  pl.BlockSpec((1, nst, SSM_SUPER), lambda b, k, c: (k, 0, 0)),
                  pl.BlockSpec((1, nst, SSM_SUPER), lambda b, k, c: (k, 0, 0)),
                  pw, pw,
                  pl.BlockSpec((1, SSM_SUPER), lambda b, k, c: (0, k))],
        out_specs=[blk, st, st],
        out_shape=[jax.ShapeDtypeStruct(u.shape, F32),
                   jax.ShapeDtypeStruct((bsz, 1, n_super * nst), F32),
                   jax.ShapeDtypeStruct((bsz, 1, n_super * nst), F32)],
        scratch_shapes=[pltpu.VMEM((SSM_SUPER // LANES, chunk, LANES), F32), pltpu.VMEM((chunk, 2 * nst), F32),
                        pltpu.VMEM((chunk, 2 * nst), BF16), pltpu.VMEM((2, nst), F32)],
        compiler_params=_params("parallel", "parallel", "arbitrary"),
        name="ssm_prompt",
    )(u, prm['wb'], prm['wc_re'], prm['wc_im'], prm['p_re'], prm['p_im'], d)
    return y, sre[:, 0], sim[:, 0]


def _ssm_step_kernel(u_ref, x0r_ref, x0i_ref, wb_ref, wcr_ref, wci_ref, pr_ref, pi_ref, d_ref,
                     y_ref, sre_ref, sim_ref):
    nst = x0r_ref.shape[1]
    u = u_ref[...]
    bu = jnp.dot(u.astype(BF16), wb_ref[0], preferred_element_type=F32)
    a_re, a_im = pr_ref[0:1, :], pi_ref[0:1, :]
    x0r, x0i = x0r_ref[...], x0i_ref[...]
    xr = a_re * x0r - a_im * x0i + bu[:, :nst]
    xi = a_re * x0i + a_im * x0r + bu[:, nst:]
    y = (jnp.dot(xr.astype(BF16), wcr_ref[0], preferred_element_type=F32)
         - jnp.dot(xi.astype(BF16), wci_ref[0], preferred_element_type=F32)
         + d_ref[...] * u)
    y_ref[...] = _gelu_tanh(y)
    sre_ref[...] = xr
    sim_ref[...] = xi


def ssm_step(u, x0_re, x0_im, prm, d):
    bd, width = u.shape
    n_super = width // SSM_SUPER
    nst = x0_re.shape[1] // n_super
    ub = pl.BlockSpec((bd, SSM_SUPER), lambda k: (0, k))
    st = pl.BlockSpec((bd, nst), lambda k: (0, k))
    pw = pl.BlockSpec((SUBLANES, nst), lambda k: (0, k))
    return pl.pallas_call(
        _ssm_step_kernel,
        grid=(n_super,),
        in_specs=[ub, st, st,
                  pl.BlockSpec((1, SSM_SUPER, 2 * nst), lambda k: (k, 0, 0)),
                  pl.BlockSpec((1, nst, SSM_SUPER), lambda k: (k, 0, 0)),
                  pl.BlockSpec((1, nst, SSM_SUPER), lambda k: (k, 0, 0)),
                  pw, pw,
                  pl.BlockSpec((1, SSM_SUPER), lambda k: (0, k))],
        out_specs=[ub, st, st],
        out_shape=[jax.ShapeDtypeStruct(u.shape, F32), jax.ShapeDtypeStruct(x0_re.shape, F32),
                   jax.ShapeDtypeStruct(x0_im.shape, F32)],
        compiler_params=_params("parallel"),
        name="ssm_step",
    )(u, x0_re, x0_im, prm['wb'], prm['wc_re'], prm['wc_im'], prm['p_re'], prm['p_im'], d)


def _run_layer(xs, mods, wts, q_formats, attend, ssm_fns):
    groups = range(len(xs))
    d = xs[0].shape[-1]
    md = lambda i, sub, kind: mods[i][:, :, sub * 3 + kind, :]
    flat = lambda a: a.reshape(-1, a.shape[-1])
    gpre, gpost = wts['norm_pre'], wts['norm_post']
    att_w = wts['w_branch_attn'].shape[0]
    ssm_w = wts['w_branch_ssm'].shape[0]

    def ffn(hs, gate_w, up_w, down_w):
        acts = matmul_swiglu(flat(hs[0]), flat(hs[1]), gate_w, up_w, down_w.shape[0])
        ys = matmul_kgrid(*acts, down_w, tk=down_w.shape[0] // 4)
        return [ys[i].reshape(hs[i].shape) for i in groups]

    def sublayer_end(xs, ys, sub, coef, with_next):
        nxt = lambda i: (gpre[sub + 1:sub + 2], md(i, sub + 1, 0), md(i, sub + 1, 1)) if with_next else None
        res = [resid_update(xs[i], ys[i], gpost[sub:sub + 1], md(i, sub, 2), coef, nxt(i)) for i in groups]
        return [r[0] for r in res], [r[1] for r in res]

    hs = [modulate(xs[i], gpre[0:1], md(i, 0, 0), md(i, 0, 1)) for i in groups]
    ys = ffn(hs, wts['ffn1_gate'], wts['ffn1_up'], wts['ffn1_down'])
    xs, hs = sublayer_end(xs, ys, 0, 0.5, True)

    both = lambda dtypes: ((1.0, dtypes), (1.0, dtypes))
    proj = matmul_segments(flat(hs[0]), flat(hs[1]), wts['w_in'], [
        (att_w, *[(scale, [dt]) for scale, dt in q_formats]), (att_w, *both([F32, BF16])),
        (att_w, *both([F32, BF16])), (ssm_w, *both([F32])), (d, *both([BF16])), (d, *both([BF16]))])
    os = attend(proj)
    branches, states = [], []
    for i in groups:
        q, k32, k16, v32, v16, u, g_att, g_ssm = proj[i]
        y_s, s_re, s_im = ssm_fns[i](u)
        branches.append((os[i], g_att, matmul_glu(flat(y_s), wts['ssm_w_glu']), g_ssm))
        states.append((k32, v32, s_re, s_im))
    merged = matmul_merge(*branches, wts['w_branch_attn'], wts['w_branch_ssm'])
    ys = matmul_ws(*merged, wts['w_out'], F32, name="w_out")
    xs, hs = sublayer_end(xs, [ys[i].reshape(xs[i].shape) for i in groups], 1, 1.0, True)

    ys = ffn(hs, wts['ffn2_gate'], wts['ffn2_up'], wts['ffn2_down'])
    xs, _ = sublayer_end(xs, ys, 2, 0.5, False)
    return xs, states


def kernel(x_prompt, x_sample, cache_k, cache_v, state_ssm_re, state_ssm_im, page_table, c_prompt, c_sample, w_mod, b_mod, norm_pre, norm_post, ffn1_gate, ffn1_up, ffn1_down, w_in, lam_q, lam_k, attn_subln, w_branch_attn, ssm_lam_re, ssm_lam_im, ssm_log_dt, ssm_b_re, ssm_b_im, ssm_c_re, ssm_c_im, ssm_d, ssm_w_glu, w_branch_ssm, w_out, ffn2_gate, ffn2_up, ffn2_down):
    depth = w_mod.shape[0]
    bsz, s_len, d = x_prompt.shape
    bd, dec_seq, _ = x_sample.shape
    assert dec_seq == 1, "the decode attention handles one new token per sequence"
    heads = cache_k.shape[3]
    d_ff = ffn1_gate.shape[2]
    d_ff_pad = _round_up(d_ff, 1024)
    groups, nst = ssm_lam_re.shape[1:]

    xp, xs = x_prompt, x_sample.reshape(1, bd, d)
    outs = [[] for _ in range(8)]
    for li in range(depth):
        lam_init = 0.8 - 0.6 * math.exp(-0.3 * li)
        wts = dict(norm_pre=norm_pre[li], norm_post=norm_post[li],
                   ffn1_gate=ffn1_gate[li], ffn1_up=ffn1_up[li], ffn1_down=cast_pad_rows(ffn1_down[li], d_ff_pad),
                   ffn2_gate=ffn2_gate[li], ffn2_up=ffn2_up[li], ffn2_down=cast_pad_rows(ffn2_down[li], d_ff_pad),
                   w_in=w_in[li].astype(BF16), w_branch_attn=w_branch_attn[li], w_branch_ssm=w_branch_ssm[li],
                   ssm_w_glu=ssm_w_glu[li].astype(BF16), w_out=w_out[li])
        ssm_prm = ssm_prepare(ssm_lam_re[li], ssm_lam_im[li], ssm_log_dt[li], ssm_b_re[li], ssm_b_im[li],
                              ssm_c_re[li], ssm_c_im[li])
        d_row = ssm_d[li].reshape(1, -1)
        subln = attn_subln[li].reshape(1, -1)

        n_cond = bsz + bd
        c_all = jnp.pad(jnp.concatenate([c_prompt, c_sample], axis=0), ((0, _round_up(n_cond, 16) - n_cond), (0, 0)))
        mod = matmul_mod(c_all, w_mod[li], b_mod[li].reshape(1, -1))
        mod_p = mod[:bsz].reshape(bsz, 1, N_SUB * 3, d)
        mod_s = mod[bsz:n_cond].reshape(1, bd, N_SUB * 3, d)

        def attend(proj):
            (q, _, k16, _, v16, *_), (qd, kd32, _, vd32, *_) = proj
            return attention(q, k16, v16, qd, kd32, vd32, cache_k[li], cache_v[li], page_table,
                             lam_q[li], lam_k[li], subln, bsz, lam_init)

        def ssm_p(u):
            return ssm_prompt(u.reshape(bsz, s_len, -1), ssm_prm, d_row)

        def ssm_s(u):
            return ssm_step(u, state_ssm_re[li].reshape(bd, -1), state_ssm_im[li].reshape(bd, -1), ssm_prm, d_row)

        q_formats = [(QK_DIM ** -0.5 * LOG2_E, BF16), (QK_DIM ** -0.5, F32)]
        (xp, xs), ((k1, v1, r1, i1), (k2, v2, r2, i2)) = _run_layer(
            [xp, xs], [mod_p, mod_s], wts, q_formats, attend, [ssm_p, ssm_s])
        new = [k1.reshape(bsz, s_len, heads, HEAD_COLS), v1.reshape(bsz, s_len, heads, HEAD_COLS),
               r1.reshape(bsz, groups, nst), i1.reshape(bsz, groups, nst),
               k2.reshape(bd, 1, heads, HEAD_COLS), v2.reshape(bd, 1, heads, HEAD_COLS),
               r2.reshape(bd, groups, nst), i2.reshape(bd, groups, nst)]
        for acc, val in zip(outs, new):
            acc.append(val)
    return (xp, xs.reshape(bd, 1, d), *[jnp.stack(o) for o in outs])
```

```python
import functools
import math

import numpy as np
import jax
import jax.numpy as jnp
from jax import lax
from jax.experimental import pallas as pl
from jax.experimental.pallas import tpu as pltpu

F32 = jnp.float32
BF16 = jnp.bfloat16
EPS = 1e-6
NEG = -1e30

LOG2_E = math.log2(math.e)
QK_DIM = 128
HEAD_COLS = 2 * QK_DIM
SSM_GROUP = 16
SSM_STATE = 64
N_SUB = 3
SUBLANES = 8
LANES = 128
SSM_SUPER = 256
SSM_SEG = 64
SOFTMAX_ROWS = 64
VMEM_LIMIT_BYTES = 56 * 1024 * 1024


def _params(*sem):
    return pltpu.CompilerParams(dimension_semantics=sem, vmem_limit_bytes=VMEM_LIMIT_BYTES)


def _round_up(x, m):
    return (x + m - 1) // m * m


def _sigmoid(x):
    return 1.0 / (1.0 + jnp.exp(-x))


def _rms(x, gain):
    return x * lax.rsqrt(jnp.mean(x * x, axis=-1, keepdims=True) + EPS) * gain


def _mm_segments_kernel(x_ref, xs_ref, w_ref, *o_refs, segs):
    i, j = pl.program_id(0), pl.program_id(1)
    pos = 0
    for start, count, (scale_m, dtypes_m), (scale_s, dtypes_s) in segs:
        outs_m = o_refs[pos:pos + len(dtypes_m)]
        outs_s = o_refs[pos + len(dtypes_m):pos + len(dtypes_m) + len(dtypes_s)]
        pos += len(dtypes_m) + len(dtypes_s)
        in_seg = jnp.logical_and(j >= start, j < start + count)

        def emit(src_ref, outs, scale):
            acc = jnp.dot(src_ref[...], w_ref[...], preferred_element_type=F32)
            if scale != 1.0:
                acc = acc * scale
            for o_ref in outs:
                o_ref[...] = acc.astype(o_ref.dtype)

        pl.when(in_seg)(functools.partial(emit, x_ref, outs_m, scale_m))
        pl.when(jnp.logical_and(in_seg, i == 0))(functools.partial(emit, xs_ref, outs_s, scale_s))


def matmul_segments(x, x_side, w, segments, *, tm=1024, tn=512):
    m, k = x.shape
    ms = x_side.shape[0]
    tm = min(tm, m)
    segs, in_tiles, out_specs, out_shape, is_side = [], 0, [], [], []
    for n_cols, fmt_m, fmt_s in segments:
        start, count = in_tiles, n_cols // tn
        segs.append((start, count, (fmt_m[0], tuple(fmt_m[1])), (fmt_s[0], tuple(fmt_s[1]))))
        in_tiles += count
        for dt in fmt_m[1]:
            out_specs.append(pl.BlockSpec(
                (tm, tn), lambda i, j, start=start, count=count: (i, jnp.clip(j - start, 0, count - 1))))
            out_shape.append(jax.ShapeDtypeStruct((m, n_cols), dt))
            is_side.append(False)
        for dt in fmt_s[1]:
            out_specs.append(pl.BlockSpec(
                (ms, tn), lambda i, j, start=start, count=count:
                (0, jnp.where(i == 0, jnp.clip(j - start, 0, count - 1), count - 1))))
            out_shape.append(jax.ShapeDtypeStruct((ms, n_cols), dt))
            is_side.append(True)
    assert in_tiles * tn == w.shape[1]
    outs = pl.pallas_call(
        functools.partial(_mm_segments_kernel, segs=tuple(segs)),
        grid=(m // tm, in_tiles),
        in_specs=[pl.BlockSpec((tm, k), lambda i, j: (i, 0)),
                  pl.BlockSpec((ms, k), lambda i, j: (0, 0)),
                  pl.BlockSpec((k, tn), lambda i, j: (0, j))],
        out_specs=out_specs, out_shape=out_shape,
        compiler_params=_params("arbitrary", "arbitrary"),
        name="w_in_segments",
    )(x, x_side, w)
    return ([o for o, s in zip(outs, is_side) if not s], [o for o, s in zip(outs, is_side) if s])


def _mm_ws_kernel(x_ref, xs_ref, w_ref, o_ref, os_ref, w_sc):
    first = pl.program_id(1) == 0
    _cast_weights_once([(w_ref, w_sc)], first)
    o_ref[...] = jnp.dot(x_ref[...], w_sc[...], preferred_element_type=F32).astype(o_ref.dtype)

    @pl.when(first)
    def _():
        os_ref[...] = jnp.dot(xs_ref[...], w_sc[...], preferred_element_type=F32).astype(os_ref.dtype)


def matmul_ws(x, x_side, w, out_dtype, *, tm=1024, tn=512, name):
    m, k = x.shape
    ms = x_side.shape[0]
    n = w.shape[1]
    tm, tn = min(tm, m), min(tn, n)
    return pl.pallas_call(
        _mm_ws_kernel,
        grid=(n // tn, m // tm),
        in_specs=[pl.BlockSpec((tm, k), lambda j, i: (i, 0)),
                  pl.BlockSpec((ms, k), lambda j, i: (0, 0)),
                  pl.BlockSpec((k, tn), lambda j, i: (0, j))],
        out_specs=[pl.BlockSpec((tm, tn), lambda j, i: (i, j)), pl.BlockSpec((ms, tn), lambda j, i: (0, j))],
        out_shape=[jax.ShapeDtypeStruct((m, n), out_dtype), jax.ShapeDtypeStruct((ms, n), out_dtype)],
        scratch_shapes=[pltpu.VMEM((k, tn), BF16)],
        compiler_params=_params("arbitrary", "arbitrary"),
        name=name,
    )(x, x_side, w)


def _cast_pad_rows_kernel(w_ref, o_ref, *, n_rows):
    tr = w_ref.shape[0]
    row = lax.broadcasted_iota(jnp.int32, w_ref.shape, 0) + pl.program_id(0) * tr
    o_ref[...] = jnp.where(row < n_rows, w_ref[...], 0.0).astype(o_ref.dtype)


def cast_pad_rows(w, n_rows_out, *, tr=512):
    r, c = w.shape
    return pl.pallas_call(
        functools.partial(_cast_pad_rows_kernel, n_rows=r),
        grid=(n_rows_out // tr,),
        in_specs=[pl.BlockSpec((tr, c), lambda i: (jnp.minimum(i, pl.cdiv(r, tr) - 1), 0))],
        out_specs=pl.BlockSpec((tr, c), lambda i: (i, 0)),
        out_shape=jax.ShapeDtypeStruct((n_rows_out, c), BF16),
        compiler_params=_params("parallel"),
        name="cast_pad_rows",
    )(w)


def _mm_kgrid_kernel(x_ref, xs_ref, w_ref, o_ref, os_ref, acc_sc, accs_sc):
    kk = pl.program_id(2)

    def accumulate(src_ref, acc_ref, dst_ref):
        part = jnp.dot(src_ref[...], w_ref[...], preferred_element_type=F32)

        @pl.when(kk == 0)
        def _():
            acc_ref[...] = part

        @pl.when(jnp.logical_and(kk != 0, kk != pl.num_programs(2) - 1))
        def _():
            acc_ref[...] += part

        @pl.when(kk == pl.num_programs(2) - 1)
        def _():
            dst_ref[...] = (acc_ref[...] + part).astype(dst_ref.dtype)

    accumulate(x_ref, acc_sc, o_ref)
    pl.when(pl.program_id(0) == 0)(functools.partial(accumulate, xs_ref, accs_sc, os_ref))


def matmul_kgrid(x, x_side, w, out_dtype, *, tm=1024, tn=1024, tk):
    m, k = x.shape
    ms = x_side.shape[0]
    n = w.shape[1]
    tm, tn = min(tm, m), min(tn, n)
    nj, nk = n // tn, k // tk
    assert nk >= 2
    return pl.pallas_call(
        _mm_kgrid_kernel,
        grid=(m // tm, nj, nk),
        in_specs=[pl.BlockSpec((tm, tk), lambda i, j, kk: (i, kk)),
                  pl.BlockSpec((ms, tk), lambda i, j, kk: (0, jnp.where(i == 0, kk, nk - 1))),
                  pl.BlockSpec((tk, tn), lambda i, j, kk: (kk, j))],
        out_specs=[pl.BlockSpec((tm, tn), lambda i, j, kk: (i, j)),
                   pl.BlockSpec((ms, tn), lambda i, j, kk: (0, jnp.where(i == 0, j, nj - 1)))],
        out_shape=[jax.ShapeDtypeStruct((m, n), out_dtype), jax.ShapeDtypeStruct((ms, n), out_dtype)],
        scratch_shapes=[pltpu.VMEM((tm, tn), F32), pltpu.VMEM((ms, tn), F32)],
        compiler_params=_params("arbitrary", "arbitrary", "arbitrary"),
        name="ffn_down",
    )(x, x_side, w)


def _cast_weights_once(pairs, do_cast):
    @pl.when(do_cast)
    def _():
        for src, dst in pairs:
            dst[...] = src[...].astype(BF16)


def _mm_swiglu_kernel(x_ref, xs_ref, wg_ref, wu_ref, o_ref, os_ref, wg_sc, wu_sc, *, n_real):
    j, i = pl.program_id(0), pl.program_id(1)
    real = j < n_real
    _cast_weights_once([(wg_ref, wg_sc), (wu_ref, wu_sc)], jnp.logical_and(i == 0, real))

    def emit(src_ref, dst_ref):
        x = src_ref[...]
        g = jnp.dot(x, wg_sc[...], preferred_element_type=F32)
        u = jnp.dot(x, wu_sc[...], preferred_element_type=F32)
        dst_ref[...] = (g * _sigmoid(g) * u).astype(dst_ref.dtype)

    def emit_zeros(dst_ref):
        dst_ref[...] = jnp.zeros(dst_ref.shape, dst_ref.dtype)

    pl.when(real)(functools.partial(emit, x_ref, o_ref))
    pl.when(jnp.logical_not(real))(functools.partial(emit_zeros, o_ref))
    pl.when(jnp.logical_and(i == 0, real))(functools.partial(emit, xs_ref, os_ref))
    pl.when(jnp.logical_and(i == 0, jnp.logical_not(real)))(functools.partial(emit_zeros, os_ref))


def matmul_swiglu(x, x_side, wg, wu, n_out, *, tm=1024, tn=256):
    m, k = x.shape
    ms = x_side.shape[0]
    n = wg.shape[1]
    tm = min(tm, m)
    n_real = n // tn
    w_spec = pl.BlockSpec((k, tn), lambda j, i: (0, jnp.minimum(j, n_real - 1)))
    return pl.pallas_call(
        functools.partial(_mm_swiglu_kernel, n_real=n_real),
        grid=(n_out // tn, m // tm),
        in_specs=[pl.BlockSpec((tm, k), lambda j, i: (jnp.where(j < n_real, i, 0), 0)),
                  pl.BlockSpec((ms, k), lambda j, i: (0, 0)), w_spec, w_spec],
        out_specs=[pl.BlockSpec((tm, tn), lambda j, i: (i, j)), pl.BlockSpec((ms, tn), lambda j, i: (0, j))],
        out_shape=[jax.ShapeDtypeStruct((m, n_out), BF16), jax.ShapeDtypeStruct((ms, n_out), BF16)],
        scratch_shapes=[pltpu.VMEM((k, tn), BF16), pltpu.VMEM((k, tn), BF16)],
        compiler_params=_params("arbitrary", "arbitrary"),
        name="ffn_up_swiglu",
    )(x, x_side, wg, wu)


def _mm_glu_kernel(x_ref, w_ref, e_ref, o_ref):
    acc = jnp.dot(x_ref[...].astype(BF16), w_ref[...], preferred_element_type=F32)
    o_ref[...] = (e_ref[...] * _sigmoid(acc)).astype(o_ref.dtype)


def matmul_glu(y, w, *, tm=1024, tn=512):
    m, k = y.shape
    tm, tn = min(tm, m), min(tn, k)
    return pl.pallas_call(
        _mm_glu_kernel,
        grid=(m // tm, k // tn),
        in_specs=[pl.BlockSpec((tm, k), lambda i, j: (i, 0)),
                  pl.BlockSpec((k, tn), lambda i, j: (0, j)),
                  pl.BlockSpec((tm, tn), lambda i, j: (i, j))],
        out_specs=pl.BlockSpec((tm, tn), lambda i, j: (i, j)),
        out_shape=jax.ShapeDtypeStruct((m, k), BF16),
        compiler_params=_params("parallel", "arbitrary"),
        name="ssm_glu",
    )(y, w, y)


def _mm_merge_kernel(xa_ref, ga_ref, xs_ref, gs_ref, xa2_ref, ga2_ref, xs2_ref, gs2_ref, wa_ref, ws_ref,
                     o_ref, o2_ref, wa_sc, ws_sc):
    first = pl.program_id(1) == 0
    _cast_weights_once([(wa_ref, wa_sc), (ws_ref, ws_sc)], first)

    def emit(xa, ga, xs, gs, dst_ref):
        ya = jnp.dot(xa[...], wa_sc[...], preferred_element_type=F32)
        ys = jnp.dot(xs[...], ws_sc[...], preferred_element_type=F32)
        merged = _sigmoid(ga[...].astype(F32)) * ya + _sigmoid(gs[...].astype(F32)) * ys
        dst_ref[...] = merged.astype(dst_ref.dtype)

    emit(xa_ref, ga_ref, xs_ref, gs_ref, o_ref)
    pl.when(first)(functools.partial(emit, xa2_ref, ga2_ref, xs2_ref, gs2_ref, o2_ref))


def matmul_merge(main, side, wa, ws, *, tm=1024, tn=512):
    m, ka = main[0].shape
    ms = side[0].shape[0]
    ks = main[2].shape[1]
    n = wa.shape[1]
    tm, tn = min(tm, m), min(tn, n)
    tile = pl.BlockSpec((tm, tn), lambda j, i: (i, j))
    tile2 = pl.BlockSpec((ms, tn), lambda j, i: (0, j))
    rows = lambda kdim: pl.BlockSpec((tm, kdim), lambda j, i: (i, 0))
    rows2 = lambda kdim: pl.BlockSpec((ms, kdim), lambda j, i: (0, 0))
    return pl.pallas_call(
        _mm_merge_kernel,
        grid=(n // tn, m // tm),
        in_specs=[rows(ka), tile, rows(ks), tile, rows2(ka), tile2, rows2(ks), tile2,
                  pl.BlockSpec((ka, tn), lambda j, i: (0, j)), pl.BlockSpec((ks, tn), lambda j, i: (0, j))],
        out_specs=[tile, tile2],
        out_shape=[jax.ShapeDtypeStruct((m, n), BF16), jax.ShapeDtypeStruct((ms, n), BF16)],
        scratch_shapes=[pltpu.VMEM((ka, tn), BF16), pltpu.VMEM((ks, tn), BF16)],
        compiler_params=_params("arbitrary", "arbitrary"),
        name="branch_merge",
    )(*main, *side, wa, ws)


def _mm_mod_kernel(c_ref, w_ref, b_ref, o_ref):
    c = c_ref[...]
    lhs = (c * _sigmoid(c)).astype(BF16)
    acc = jnp.dot(lhs, w_ref[...].astype(BF16), preferred_element_type=F32)
    o_ref[...] = acc + b_ref[...]


def matmul_mod(c, w, b, *, tn=512):
    m, k = c.shape
    n = w.shape[1]
    return pl.pallas_call(
        _mm_mod_kernel,
        grid=(n // tn,),
        in_specs=[pl.BlockSpec((m, k), lambda j: (0, 0)),
                  pl.BlockSpec((k, tn), lambda j: (0, j)),
                  pl.BlockSpec((1, tn), lambda j: (0, j))],
        out_specs=pl.BlockSpec((m, tn), lambda j: (0, j)),
        out_shape=jax.ShapeDtypeStruct((m, n), F32),
        compiler_params=_params("arbitrary"),
        name="adaln_mod",
    )(c, w, b)


def _modulate_kernel(x_ref, g_ref, sh_ref, sc_ref, h_ref):
    h = _rms(x_ref[0], g_ref[...]) * (1.0 + sc_ref[0]) + sh_ref[0]
    h_ref[0] = h.astype(h_ref.dtype)


def _resid_kernel(x_ref, y_ref, gpost_ref, gate_ref, *rest, coef, with_next):
    xn = x_ref[0] + coef * gate_ref[0] * _rms(y_ref[0].astype(F32), gpost_ref[...])
    if with_next:
        gpre_ref, sh_ref, sc_ref, xo_ref, h_ref = rest
        h = _rms(xn, gpre_ref[...]) * (1.0 + sc_ref[0]) + sh_ref[0]
        h_ref[0] = h.astype(h_ref.dtype)
    else:
        (xo_ref,) = rest
    xo_ref[0] = xn


def _row_specs(x, mod_rows, tr):
    g, t, d = x.shape
    tr = min(tr, t)
    row = pl.BlockSpec((1, tr, d), lambda gi, ti: (gi, ti, 0))
    gain = pl.BlockSpec((1, d), lambda gi, ti: (0, 0))
    if mod_rows == 1:
        mod = pl.BlockSpec((1, 1, d), lambda gi, ti: (gi, 0, 0))
    else:
        mod = pl.BlockSpec((1, tr, d), lambda gi, ti: (gi, ti, 0))
    return (g, t // tr), row, gain, mod


def modulate(x, gain, shift, scale, *, tr=256):
    grid, row, gspec, mod = _row_specs(x, shift.shape[1], tr)
    return pl.pallas_call(
        _modulate_kernel, grid=grid,
        in_specs=[row, gspec, mod, mod], out_specs=row,
        out_shape=jax.ShapeDtypeStruct(x.shape, BF16),
        compiler_params=_params("parallel", "parallel"),
        name="modulate",
    )(x, gain, shift, scale)


def resid_update(x, y, gpost, gate, coef, nxt=None, *, tr=256):
    grid, row, gspec, mod = _row_specs(x, gate.shape[1], tr)
    in_specs = [row, row, gspec, mod]
    args = [x, y, gpost, gate]
    out_specs = [row]
    out_shape = [jax.ShapeDtypeStruct(x.shape, F32)]
    if nxt is not None:
        in_specs += [gspec, mod, mod]
        args += list(nxt)
        out_specs.append(row)
        out_shape.append(jax.ShapeDtypeStruct(x.shape, BF16))
    outs = pl.pallas_call(
        functools.partial(_resid_kernel, coef=coef, with_next=nxt is not None), grid=grid,
        in_specs=in_specs, out_specs=out_specs, out_shape=out_shape,
        compiler_params=_params("parallel", "parallel"),
        name="resid_update",
    )(*args)
    return outs if nxt is not None else (outs[0], None)


def _diff_lambda(lamq_ref, lamk_ref, lam_init):
    prod = lamq_ref[...] * lamk_ref[...]
    s0 = jnp.sum(prod[0:1], axis=-1, keepdims=True)
    s1 = jnp.sum(prod[1:2], axis=-1, keepdims=True)
    return jnp.exp(s0) - jnp.exp(s1) + lam_init


def _decode_pages(qm_bf, k_refs, v_refs, state):
    m_prev, l_prev, acc_prev = state
    page, heads, _ = k_refs[0].shape[1:]
    n_rows, n_keys = 2 * heads, page * heads
    row = lax.broadcasted_iota(jnp.int32, (n_rows, n_keys), 0)
    col = lax.broadcasted_iota(jnp.int32, (n_rows, n_keys), 1)
    same_head = (col & (heads - 1)) == (row & (heads - 1))
    scores = []
    for k_ref in k_refs:
        k2 = k_ref[0].reshape(n_keys, HEAD_COLS).astype(BF16)
        s = lax.dot_general(qm_bf, k2, (((1,), (1,)), ((), ())), preferred_element_type=F32)
        scores.append(jnp.where(same_head, s, NEG))
    m_new = m_prev
    for s in scores:
        m_new = jnp.maximum(m_new, jnp.max(s, axis=-1, keepdims=True))
    corr = jnp.exp(m_prev - m_new)
    l_new = l_prev * corr
    acc = acc_prev * corr
    for s, v_ref in zip(scores, v_refs):
        p = jnp.exp(s - m_new)
        l_new = l_new + jnp.sum(p, axis=-1, keepdims=True)
        v2 = v_ref[0].reshape(n_keys, HEAD_COLS).astype(BF16)
        acc = acc + jnp.dot(p.astype(BF16), v2, preferred_element_type=F32)
    return m_new, l_new, acc


def _attn_kernel(qi_tab, ki_tab, pt_ref, lamq_ref, lamk_ref, subln_ref, q_ref, k_ref, v_ref,
                 qm_ref, kn_ref, vn_ref, *rest, n_pp, lam_init, steps_per_seq, n_dec_steps):
    k_refs, v_refs = rest[:n_pp], rest[n_pp:2 * n_pp]
    o_ref, od_ref, m_sc, l_sc, acc_sc, s_sc, p_sc, corr_sc, dm_sc, dl_sc, dacc_sc = rest[2 * n_pp:]
    t = pl.program_id(2)
    qi, ki = qi_tab[t], ki_tab[t]
    tq, tk = q_ref.shape[0], k_ref.shape[0]
    heads = kn_ref.shape[1]
    step = (pl.program_id(0) * pl.num_programs(1) + pl.program_id(1)) * pl.num_programs(2) + t
    page_group = lax.rem(step, steps_per_seq)
    qm = qm_ref[0]

    @pl.when(ki == 0)
    def _():
        m_sc[...] = jnp.full(m_sc.shape, NEG, F32)
        l_sc[...] = jnp.zeros(l_sc.shape, F32)
        acc_sc[...] = jnp.zeros(acc_sc.shape, F32)

    def advance(diagonal):
        q, k, v = q_ref[...], k_ref[...], v_ref[...]
        for mp in range(2):
            cols = slice(mp * QK_DIM, (mp + 1) * QK_DIM)
            s_sc[mp] = lax.dot_general(q[:, cols], k[:, cols], (((1,), (1,)), ((), ())),
                                       preferred_element_type=F32)
        for mp in range(2):
            for r0 in range(0, tq, SOFTMAX_ROWS):
                rows = slice(r0, r0 + SOFTMAX_ROWS)
                s = s_sc[mp, rows, :]
                if diagonal:
                    row = lax.broadcasted_iota(jnp.int32, s.shape, 0) + r0
                    col = lax.broadcasted_iota(jnp.int32, s.shape, 1)
                    s = jnp.where(col <= row, s, NEG)
                m_prev = m_sc[mp, rows]
                m_new = jnp.maximum(m_prev, jnp.max(s, axis=-1, keepdims=True))
                corr = jnp.exp2(m_prev - m_new)
                p = jnp.exp2(s - m_new)
                l_sc[mp, rows] = l_sc[mp, rows] * corr + jnp.sum(p, axis=-1, keepdims=True)
                m_sc[mp, rows] = m_new
                corr_sc[mp, rows] = corr
                p_sc[mp, rows, :] = p.astype(BF16)
        for mp in range(2):
            acc_sc[mp] = acc_sc[mp] * corr_sc[mp] + jnp.dot(p_sc[mp], v, preferred_element_type=F32)
        fresh = page_group == 0
        state = (jnp.where(fresh, NEG, dm_sc[...]), jnp.where(fresh, 0.0, dl_sc[...]),
                 jnp.where(fresh, 0.0, dacc_sc[...]))
        dm_sc[...], dl_sc[...], dacc_sc[...] = _decode_pages(qm.astype(BF16), k_refs, v_refs, state)

    @pl.when(ki < qi)
    def _():
        advance(False)

    @pl.when(ki == qi)
    def _():
        advance(True)
        lam = _diff_lambda(lamq_ref, lamk_ref, lam_init)
        o = acc_sc[0] / l_sc[0] - lam * (acc_sc[1] / l_sc[1])
        o_ref[...] = (_rms(o, subln_ref[...]) * (1.0 - lam_init)).astype(o_ref.dtype)

    @pl.when(jnp.logical_and(step < n_dec_steps, page_group == steps_per_seq - 1))
    def _():
        k_new = jnp.concatenate([kn_ref[0], kn_ref[0]], axis=0)
        v_new = jnp.concatenate([vn_ref[0], vn_ref[0]], axis=0)
        s_new = jnp.sum(qm * k_new, axis=-1, keepdims=True)
        m_prev = dm_sc[...]
        m_new = jnp.maximum(m_prev, s_new)
        corr = jnp.exp(m_prev - m_new)
        p_new = jnp.exp(s_new - m_new)
        w = (dacc_sc[...] * corr + p_new * v_new) / (dl_sc[...] * corr + p_new)
        lam = _diff_lambda(lamq_ref, lamk_ref, lam_init)
        o = w[:heads] - lam * w[heads:]
        od_ref[0] = (_rms(o, subln_ref[...]) * (1.0 - lam_init)).astype(od_ref.dtype)


def attention(q, k, v, qd, kd_new, vd_new, cache_k, cache_v, page_table, lam_q, lam_k, subln, bsz, lam_init,
              *, tq=512, min_pages_per_step=4):
    m, width = q.shape
    s_len = m // bsz
    heads = width // HEAD_COLS
    assert heads == SUBLANES, "cache pages are viewed as (page*heads, 256) row tiles"
    tq = min(tq, s_len)
    nq = s_len // tq
    pairs = [(a, b) for a in range(nq) for b in range(a + 1)]
    qi_tab = jnp.asarray(np.array([p[0] for p in pairs], np.int32))
    ki_tab = jnp.asarray(np.array([p[1] for p in pairs], np.int32))
    n_steps = bsz * heads * len(pairs)

    bd, n_pages = page_table.shape
    page = cache_k.shape[1]
    n_pp = next(c for c in range(min_pages_per_step, n_pages + 1)
                if n_pages % c == 0 and bd * (n_pages // c) <= n_steps)
    steps_per_seq = n_pages // n_pp
    n_dec_steps = bd * steps_per_seq
    q4 = qd.reshape(bd, heads, 2, QK_DIM).transpose(0, 2, 1, 3)
    qm = jnp.einsum('bmhd,mn->bmhnd', q4, jnp.eye(2, dtype=F32)).reshape(bd, 2 * heads, HEAD_COLS)

    def dec_step(b, h, t):
        return jnp.minimum((b * heads + h) * len(pairs) + t, n_dec_steps - 1)

    small = lambda shape: pl.BlockSpec(shape, lambda b, h, t, qt, kt, pt: (0, 0))
    tok = lambda rows: pl.BlockSpec(
        (1, rows, HEAD_COLS), lambda b, h, t, qt, kt, pt: (dec_step(b, h, t) // steps_per_seq, 0, 0))

    def page_spec(j):
        def index(b, h, t, qt, kt, pt):
            n = dec_step(b, h, t)
            return (pt[n // steps_per_seq, lax.rem(n, steps_per_seq) * n_pp + j], 0, 0, 0)
        return pl.BlockSpec((1, page, heads, HEAD_COLS), index)

    q_blk = pl.BlockSpec((tq, HEAD_COLS), lambda b, h, t, qt, kt, pt: (b * nq + qt[t], h))
    kv_blk = pl.BlockSpec((tq, HEAD_COLS), lambda b, h, t, qt, kt, pt: (b * nq + kt[t], h))
    grid_spec = pltpu.PrefetchScalarGridSpec(
        num_scalar_prefetch=3,
        grid=(bsz, heads, len(pairs)),
        in_specs=[small(lam_q.shape), small(lam_k.shape), small(subln.shape), q_blk, kv_blk, kv_blk,
                  tok(2 * heads), tok(heads), tok(heads)] + [page_spec(j) for j in range(n_pp)] * 2,
        out_specs=[q_blk, tok(heads)],
        scratch_shapes=[pltpu.VMEM((2, tq, 1), F32), pltpu.VMEM((2, tq, 1), F32),
                        pltpu.VMEM((2, tq, HEAD_COLS), F32),
                        pltpu.VMEM((2, tq, tq), F32), pltpu.VMEM((2, tq, tq), BF16), pltpu.VMEM((2, tq, 1), F32),
                        pltpu.VMEM((2 * heads, 1), F32), pltpu.VMEM((2 * heads, 1), F32),
                        pltpu.VMEM((2 * heads, HEAD_COLS), F32)])
    o, od = pl.pallas_call(
        functools.partial(_attn_kernel, n_pp=n_pp, lam_init=lam_init, steps_per_seq=steps_per_seq,
                          n_dec_steps=n_dec_steps),
        grid_spec=grid_spec,
        out_shape=[jax.ShapeDtypeStruct((m, width), BF16), jax.ShapeDtypeStruct((bd, heads, HEAD_COLS), BF16)],
        compiler_params=_params("arbitrary", "arbitrary", "arbitrary"),
        name="attention",
    )(qi_tab, ki_tab, page_table, lam_q, lam_k, subln, q, k, v, qm,
      kd_new.reshape(bd, heads, HEAD_COLS), vd_new.reshape(bd, heads, HEAD_COLS),
      *([cache_k] * n_pp), *([cache_v] * n_pp))
    return o, od.reshape(bd, width)


def _ssm_input_kernel(lr_ref, li_ref, ldt_ref, br_ref, bi_ref, bbr_ref, bbi_ref):
    lr, li = lr_ref[...], li_ref[...]
    dt = jnp.exp(ldt_ref[...])
    mag = jnp.exp(dt * lr)
    a_re, a_im = mag * jnp.cos(dt * li), mag * jnp.sin(dt * li)
    den = lr * lr + li * li
    f_re = ((a_re - 1.0) * lr + a_im * li) / den
    f_im = (a_im * lr - (a_re - 1.0) * li) / den
    br, bi = br_ref[...], bi_ref[...]
    bbr_ref[...] = f_re * br - f_im * bi
    bbi_ref[...] = f_re * bi + f_im * br


def _ssm_power_kernel(lr_ref, li_ref, ldt_ref, pr_ref, pi_ref):
    steps = (lax.broadcasted_iota(jnp.int32, pr_ref.shape, 0) + 1).astype(F32)
    dt = jnp.exp(ldt_ref[...])
    mag = jnp.exp(steps * (dt * lr_ref[...]))
    ang = steps * (dt * li_ref[...])
    pr_ref[...] = mag * jnp.cos(ang)
    pi_ref[...] = mag * jnp.sin(ang)


def ssm_prepare(lam_re, lam_im, log_dt, b_re, b_im, c_re, c_im):
    groups, nst = lam_re.shape
    gps = SSM_SUPER // SSM_GROUP
    n_super = groups // gps
    rep = lambda a: jnp.repeat(a, SSM_GROUP, axis=0)
    ldt_gn = jnp.broadcast_to(log_dt[:, None], (groups, nst))
    full = lambda a: pl.BlockSpec(a.shape, lambda: (0,) * a.ndim)
    bt = lambda b: b.transpose(0, 2, 1).reshape(groups * SSM_GROUP, nst)
    ins = [rep(lam_re), rep(lam_im), rep(ldt_gn), bt(b_re), bt(b_im)]
    bbr, bbi = pl.pallas_call(
        _ssm_input_kernel,
        in_specs=[full(a) for a in ins], out_specs=[full(ins[0])] * 2,
        out_shape=[jax.ShapeDtypeStruct(ins[0].shape, F32)] * 2,
    )(*ins)
    flat = lambda a: a.reshape(1, groups * nst)
    ins = [flat(lam_re), flat(lam_im), flat(ldt_gn)]
    pw = jax.ShapeDtypeStruct((SSM_SEG, groups * nst), F32)
    p_re, p_im = pl.pallas_call(
        _ssm_power_kernel,
        in_specs=[full(a) for a in ins], out_specs=[pl.BlockSpec(pw.shape, lambda: (0, 0))] * 2,
        out_shape=[pw, pw],
    )(*ins)
    eye = jnp.eye(gps, dtype=F32)

    def in_blocks(bb):
        x = bb.reshape(n_super, gps, SSM_GROUP, nst)
        return jnp.einsum('kgsn,gh->kgshn', x, eye).reshape(n_super, SSM_SUPER, gps * nst)

    def out_blocks(c):
        x = c.reshape(n_super, gps, SSM_GROUP, nst)
        return jnp.einsum('kgsn,gh->kgnhs', x, eye).reshape(n_super, gps * nst, SSM_SUPER).astype(BF16)

    wb = jnp.concatenate([in_blocks(bbr), in_blocks(bbi)], axis=-1).astype(BF16)
    return dict(wb=wb, wc_re=out_blocks(c_re), wc_im=out_blocks(c_im), p_re=p_re, p_im=p_im)


def _gelu_tanh(x):
    return 0.5 * x * (1.0 + jnp.tanh(math.sqrt(2.0 / math.pi) * (x + 0.044715 * (x * x * x))))


def _ssm_prompt_kernel(u_ref, wb_ref, wcr_ref, wci_ref, pr_ref, pi_ref, d_ref, y_ref, sre_ref, sim_ref,
                       perm_sc, bu_sc, xb_sc, car_sc):
    c = pl.program_id(2)
    seg = pr_ref.shape[0]
    nst = pr_ref.shape[1]
    re, im = slice(0, nst), slice(nst, 2 * nst)

    @pl.when(c == 0)
    def _():
        car_sc[...] = jnp.zeros(car_sc.shape, F32)

    n_lane_tiles = perm_sc.shape[0]
    for s in range(SUBLANES):
        for lt in range(n_lane_tiles):
            perm_sc[lt, pl.ds(s, seg, stride=SUBLANES), :] = (
                u_ref[0, s * seg:(s + 1) * seg, lt * LANES:(lt + 1) * LANES])
    up = jnp.concatenate([perm_sc[lt] for lt in range(n_lane_tiles)], axis=-1)
    bu_sc[...] = jnp.dot(up.astype(BF16), wb_ref[0], preferred_element_type=F32)

    a_re = jnp.broadcast_to(pr_ref[0:1, :], (SUBLANES, nst))
    a_im = jnp.broadcast_to(pi_ref[0:1, :], (SUBLANES, nst))

    def scan_body(i, carry):
        xr, xi = carry
        rows = pl.ds(pl.multiple_of(i * SUBLANES, SUBLANES), SUBLANES)
        nr = a_re * xr - a_im * xi + bu_sc[rows, re]
        ni = a_re * xi + a_im * xr + bu_sc[rows, im]
        bu_sc[rows, re] = nr
        bu_sc[rows, im] = ni
        return nr, ni

    zero = jnp.zeros((SUBLANES, nst), F32)
    er, ei = lax.fori_loop(0, seg, scan_body, (zero, zero), unroll=4)

    s_re, s_im = pr_ref[seg - 1:seg, :], pi_ref[seg - 1:seg, :]
    hr, hi = car_sc[0:1, :], car_sc[1:2, :]
    starts_r, starts_i = [], []
    for s in range(SUBLANES):
        starts_r.append(hr)
        starts_i.append(hi)
        hr, hi = (s_re * hr - s_im * hi + er[s:s + 1], s_re * hi + s_im * hr + ei[s:s + 1])
    car_sc[0:1, :] = hr
    car_sc[1:2, :] = hi
    h0r = jnp.concatenate(starts_r, axis=0)
    h0i = jnp.concatenate(starts_i, axis=0)

    def fix_body(i2, _):
        halves_r, halves_i = [], []
        for half in range(2):
            i = i2 * 2 + half
            rows = pl.ds(pl.multiple_of(i * SUBLANES, SUBLANES), SUBLANES)
            pr, pi = pr_ref[pl.ds(i, 1), :], pi_ref[pl.ds(i, 1), :]
            halves_r.append(bu_sc[rows, re] + (pr * h0r - pi * h0i))
            halves_i.append(bu_sc[rows, im] + (pr * h0i + pi * h0r))
        rows16 = pl.ds(pl.multiple_of(i2 * 2 * SUBLANES, 2 * SUBLANES), 2 * SUBLANES)
        xb_sc[rows16, re] = jnp.concatenate(halves_r, axis=0).astype(BF16)
        xb_sc[rows16, im] = jnp.concatenate(halves_i, axis=0).astype(BF16)
        return 0

    lax.fori_loop(0, seg // 2, fix_body, 0, unroll=2)

    y = (jnp.dot(xb_sc[:, re], wcr_ref[0], preferred_element_type=F32)
         - jnp.dot(xb_sc[:, im], wci_ref[0], preferred_element_type=F32)
         + d_ref[...] * up)
    y = _gelu_tanh(y)
    for lt in range(n_lane_tiles):
        perm_sc[lt] = y[:, lt * LANES:(lt + 1) * LANES]
    for s in range(SUBLANES):
        for lt in range(n_lane_tiles):
            y_ref[0, s * seg:(s + 1) * seg, lt * LANES:(lt + 1) * LANES] = (
                perm_sc[lt, pl.ds(s, seg, stride=SUBLANES), :])

    @pl.when(c == pl.num_programs(2) - 1)
    def _():
        sre_ref[0] = hr
        sim_ref[0] = hi


def ssm_prompt(u, prm, d):
    bsz, s_len, width = u.shape
    n_super = width // SSM_SUPER
    nst = prm['p_re'].shape[1] // n_super
    chunk = SUBLANES * SSM_SEG
    blk = pl.BlockSpec((1, chunk, SSM_SUPER), lambda b, k, c: (b, c, k))
    st = pl.BlockSpec((1, 1, nst), lambda b, k, c: (b, 0, k))
    pw = pl.BlockSpec((SSM_SEG, nst), lambda b, k, c: (0, k))
    y, sre, sim = pl.pallas_call(
        _ssm_prompt_kernel,
        grid=(bsz, n_super, s_len // chunk),
        in_specs=[blk,
                  pl.BlockSpec((1, SSM_SUPER, 2 * nst), lambda b, k, c: (k, 0, 0)),
                  pl.BlockSpec((1, nst, SSM_SUPER), lambda b, k, c: (k, 0, 0)),
                  pl.BlockSpec((1, nst, SSM_SUPER), lambda b, k, c: (k, 0, 0)),
                  pw, pw,
                  pl.BlockSpec((1, SSM_SUPER), lambda b, k, c: (0, k))],
        out_specs=[blk, st, st],
        out_shape=[jax.ShapeDtypeStruct(u.shape, F32),
                   jax.ShapeDtypeStruct((bsz, 1, n_super * nst), F32),
                   jax.ShapeDtypeStruct((bsz, 1, n_super * nst), F32)],
        scratch_shapes=[pltpu.VMEM((SSM_SUPER // LANES, chunk, LANES), F32), pltpu.VMEM((chunk, 2 * nst), F32),
                        pltpu.VMEM((chunk, 2 * nst), BF16), pltpu.VMEM((2, nst), F32)],
        compiler_params=_params("parallel", "parallel", "arbitrary"),
        name="ssm_prompt",
    )(u, prm['wb'], prm['wc_re'], prm['wc_im'], prm['p_re'], prm['p_im'], d)
    return y, sre[:, 0], sim[:, 0]


def _ssm_step_kernel(u_ref, x0r_ref, x0i_ref, wb_ref, wcr_ref, wci_ref, pr_ref, pi_ref, d_ref,
                     y_ref, sre_ref, sim_ref):
    nst = x0r_ref.shape[1]
    u = u_ref[...]
    bu = jnp.dot(u.astype(BF16), wb_ref[0], preferred_element_type=F32)
    a_re, a_im = pr_ref[0:1, :], pi_ref[0:1, :]
    x0r, x0i = x0r_ref[...], x0i_ref[...]
    xr = a_re * x0r - a_im * x0i + bu[:, :nst]
    xi = a_re * x0i + a_im * x0r + bu[:, nst:]
    y = (jnp.dot(xr.astype(BF16), wcr_ref[0], preferred_element_type=F32)
         - jnp.dot(xi.astype(BF16), wci_ref[0], preferred_element_type=F32)
         + d_ref[...] * u)
    y_ref[...] = _gelu_tanh(y)
    sre_ref[...] = xr
    sim_ref[...] = xi


def ssm_step(u, x0_re, x0_im, prm, d):
    bd, width = u.shape
    n_super = width // SSM_SUPER
    nst = x0_re.shape[1] // n_super
    ub = pl.BlockSpec((bd, SSM_SUPER), lambda k: (0, k))
    st = pl.BlockSpec((bd, nst), lambda k: (0, k))
    pw = pl.BlockSpec((SUBLANES, nst), lambda k: (0, k))
    return pl.pallas_call(
        _ssm_step_kernel,
        grid=(n_super,),
        in_specs=[ub, st, st,
                  pl.BlockSpec((1, SSM_SUPER, 2 * nst), lambda k: (k, 0, 0)),
                  pl.BlockSpec((1, nst, SSM_SUPER), lambda k: (k, 0, 0)),
                  pl.BlockSpec((1, nst, SSM_SUPER), lambda k: (k, 0, 0)),
                  pw, pw,
                  pl.BlockSpec((1, SSM_SUPER), lambda k: (0, k))],
        out_specs=[ub, st, st],
        out_shape=[jax.ShapeDtypeStruct(u.shape, F32), jax.ShapeDtypeStruct(x0_re.shape, F32),
                   jax.ShapeDtypeStruct(x0_im.shape, F32)],
        compiler_params=_params("parallel"),
        name="ssm_step",
    )(u, x0_re, x0_im, prm['wb'], prm['wc_re'], prm['wc_im'], prm['p_re'], prm['p_im'], d)


def _run_layer(xs, mods, wts, q_formats, attend, ssm_fns):
    groups = range(len(xs))
    d = xs[0].shape[-1]
    md = lambda i, sub, kind: mods[i][:, :, sub * 3 + kind, :]
    flat = lambda a: a.reshape(-1, a.shape[-1])
    gpre, gpost = wts['norm_pre'], wts['norm_post']
    att_w = wts['w_branch_attn'].shape[0]
    ssm_w = wts['w_branch_ssm'].shape[0]

    def ffn(hs, gate_w, up_w, down_w):
        acts = matmul_swiglu(flat(hs[0]), flat(hs[1]), gate_w, up_w, down_w.shape[0])
        ys = matmul_kgrid(*acts, down_w, BF16, tk=down_w.shape[0] // 4)
        return [ys[i].reshape(hs[i].shape) for i in groups]

    def sublayer_end(xs, ys, sub, coef, with_next):
        nxt = lambda i: (gpre[sub + 1:sub + 2], md(i, sub + 1, 0), md(i, sub + 1, 1)) if with_next else None
        res = [resid_update(xs[i], ys[i], gpost[sub:sub + 1], md(i, sub, 2), coef, nxt(i)) for i in groups]
        return [r[0] for r in res], [r[1] for r in res]

    hs = [modulate(xs[i], gpre[0:1], md(i, 0, 0), md(i, 0, 1)) for i in groups]
    ys = ffn(hs, wts['ffn1_gate'], wts['ffn1_up'], wts['ffn1_down'])
    xs, hs = sublayer_end(xs, ys, 0, 0.5, True)

    both = lambda dtypes: ((1.0, dtypes), (1.0, dtypes))
    proj = matmul_segments(flat(hs[0]), flat(hs[1]), wts['w_in'], [
        (att_w, *[(scale, [dt]) for scale, dt in q_formats]), (att_w, *both([F32, BF16])),
        (att_w, *both([F32, BF16])), (ssm_w, *both([F32])), (d, *both([BF16])), (d, *both([BF16]))])
    os = attend(proj)
    branches, states = [], []
    for i in groups:
        q, k32, k16, v32, v16, u, g_att, g_ssm = proj[i]
        y_s, s_re, s_im = ssm_fns[i](u)
        branches.append((os[i], g_att, matmul_glu(flat(y_s), wts['ssm_w_glu']), g_ssm))
        states.append((k32, v32, s_re, s_im))
    merged = matmul_merge(*branches, wts['w_branch_attn'], wts['w_branch_ssm'])
    ys = matmul_ws(*merged, wts['w_out'], BF16, name="w_out")
    xs, hs = sublayer_end(xs, [ys[i].reshape(xs[i].shape) for i in groups], 1, 1.0, True)

    ys = ffn(hs, wts['ffn2_gate'], wts['ffn2_up'], wts['ffn2_down'])
    xs, _ = sublayer_end(xs, ys, 2, 0.5, False)
    return xs, states


def kernel(x_prompt, x_sample, cache_k, cache_v, state_ssm_re, state_ssm_im, page_table, c_prompt, c_sample, w_mod, b_mod, norm_pre, norm_post, ffn1_gate, ffn1_up, ffn1_down, w_in, lam_q, lam_k, attn_subln, w_branch_attn, ssm_lam_re, ssm_lam_im, ssm_log_dt, ssm_b_re, ssm_b_im, ssm_c_re, ssm_c_im, ssm_d, ssm_w_glu, w_branch_ssm, w_out, ffn2_gate, ffn2_up, ffn2_down):
    depth = w_mod.shape[0]
    bsz, s_len, d = x_prompt.shape
    bd, dec_seq, _ = x_sample.shape
    assert dec_seq == 1, "the decode attention handles one new token per sequence"
    heads = cache_k.shape[3]
    d_ff = ffn1_gate.shape[2]
    d_ff_pad = _round_up(d_ff, 1024)
    groups, nst = ssm_lam_re.shape[1:]

    xp, xs = x_prompt, x_sample.reshape(1, bd, d)
    outs = [[] for _ in range(8)]
    for li in range(depth):
        lam_init = 0.8 - 0.6 * math.exp(-0.3 * li)
        wts = dict(norm_pre=norm_pre[li], norm_post=norm_post[li],
                   ffn1_gate=ffn1_gate[li], ffn1_up=ffn1_up[li], ffn1_down=cast_pad_rows(ffn1_down[li], d_ff_pad),
                   ffn2_gate=ffn2_gate[li], ffn2_up=ffn2_up[li], ffn2_down=cast_pad_rows(ffn2_down[li], d_ff_pad),
                   w_in=w_in[li].astype(BF16), w_branch_attn=w_branch_attn[li], w_branch_ssm=w_branch_ssm[li],
                   ssm_w_glu=ssm_w_glu[li].astype(BF16), w_out=w_out[li])
        ssm_prm = ssm_prepare(ssm_lam_re[li], ssm_lam_im[li], ssm_log_dt[li], ssm_b_re[li], ssm_b_im[li],
                              ssm_c_re[li], ssm_c_im[li])
        d_row = ssm_d[li].reshape(1, -1)
        subln = attn_subln[li].reshape(1, -1)

        n_cond = bsz + bd
        c_all = jnp.pad(jnp.concatenate([c_prompt, c_sample], axis=0), ((0, _round_up(n_cond, 16) - n_cond), (0, 0)))
        mod = matmul_mod(c_all, w_mod[li], b_mod[li].reshape(1, -1))
        mod_p = mod[:bsz].reshape(bsz, 1, N_SUB * 3, d)
        mod_s = mod[bsz:n_cond].reshape(1, bd, N_SUB * 3, d)

        def attend(proj):
            (q, _, k16, _, v16, *_), (qd, kd32, _, vd32, *_) = proj
            return attention(q, k16, v16, qd, kd32, vd32, cache_k[li], cache_v[li], page_table,
                             lam_q[li], lam_k[li], subln, bsz, lam_init)

        def ssm_p(u):
            return ssm_prompt(u.reshape(bsz, s_len, -1), ssm_prm, d_row)

        def ssm_s(u):
            return ssm_step(u, state_ssm_re[li].reshape(bd, -1), state_ssm_im[li].reshape(bd, -1), ssm_prm, d_row)

        q_formats = [(QK_DIM ** -0.5 * LOG2_E, BF16), (QK_DIM ** -0.5, F32)]
        (xp, xs), ((k1, v1, r1, i1), (k2, v2, r2, i2)) = _run_layer(
            [xp, xs], [mod_p, mod_s], wts, q_formats, attend, [ssm_p, ssm_s])
        new = [k1.reshape(bsz, s_len, heads, HEAD_COLS), v1.reshape(bsz, s_len, heads, HEAD_COLS),
               r1.reshape(bsz, groups, nst), i1.reshape(bsz, groups, nst),
               k2.reshape(bd, 1, heads, HEAD_COLS), v2.reshape(bd, 1, heads, HEAD_COLS),
               r2.reshape(bd, groups, nst), i2.reshape(bd, groups, nst)]
        for acc, val in zip(outs, new):
            acc.append(val)
    return (xp, xs.reshape(bd, 1, d), *[jnp.stack(o) for o in outs])
```

```python
import functools
import math

import numpy as np
import jax
import jax.numpy as jnp
from jax import lax
from jax.experimental import pallas as pl
from jax.experimental.pallas import tpu as pltpu

F32 = jnp.float32
BF16 = jnp.bfloat16
EPS = 1e-6
NEG = -1e30

LOG2_E = math.log2(math.e)
QK_DIM = 128
HEAD_COLS = 2 * QK_DIM
SSM_GROUP = 16
SSM_STATE = 64
N_SUB = 3
SUBLANES = 8
LANES = 128
SSM_SUPER = 256
SSM_SEG = 64
SOFTMAX_ROWS = 64
VMEM_LIMIT_BYTES = 56 * 1024 * 1024


def _params(*sem):
    return pltpu.CompilerParams(dimension_semantics=sem, vmem_limit_bytes=VMEM_LIMIT_BYTES)


def _round_up(x, m):
    return (x + m - 1) // m * m


def _sigmoid(x):
    return 0.5 * jnp.tanh(0.5 * x) + 0.5


def _rms(x, gain):
    return x * lax.rsqrt(jnp.mean(x * x, axis=-1, keepdims=True) + EPS) * gain


def _mm_segments_kernel(x_ref, xs_ref, w_ref, *o_refs, segs):
    i, j = pl.program_id(0), pl.program_id(1)
    pos = 0
    for start, count, (scale_m, dtypes_m), (scale_s, dtypes_s) in segs:
        outs_m = o_refs[pos:pos + len(dtypes_m)]
        outs_s = o_refs[pos + len(dtypes_m):pos + len(dtypes_m) + len(dtypes_s)]
        pos += len(dtypes_m) + len(dtypes_s)
        in_seg = jnp.logical_and(j >= start, j < start + count)

        def emit(src_ref, outs, scale):
            acc = jnp.dot(src_ref[...], w_ref[...], preferred_element_type=F32)
            if scale != 1.0:
                acc = acc * scale
            for o_ref in outs:
                o_ref[...] = acc.astype(o_ref.dtype)

        pl.when(in_seg)(functools.partial(emit, x_ref, outs_m, scale_m))
        pl.when(jnp.logical_and(in_seg, i == 0))(functools.partial(emit, xs_ref, outs_s, scale_s))


def matmul_segments(x, x_side, w, segments, *, tm=1024, tn=512):
    m, k = x.shape
    ms = x_side.shape[0]
    tm = min(tm, m)
    segs, in_tiles, out_specs, out_shape, is_side = [], 0, [], [], []
    for n_cols, fmt_m, fmt_s in segments:
        start, count = in_tiles, n_cols // tn
        segs.append((start, count, (fmt_m[0], tuple(fmt_m[1])), (fmt_s[0], tuple(fmt_s[1]))))
        in_tiles += count
        for dt in fmt_m[1]:
            out_specs.append(pl.BlockSpec(
                (tm, tn), lambda i, j, start=start, count=count: (i, jnp.clip(j - start, 0, count - 1))))
            out_shape.append(jax.ShapeDtypeStruct((m, n_cols), dt))
            is_side.append(False)
        for dt in fmt_s[1]:
            out_specs.append(pl.BlockSpec(
                (ms, tn), lambda i, j, start=start, count=count:
                (0, jnp.where(i == 0, jnp.clip(j - start, 0, count - 1), count - 1))))
            out_shape.append(jax.ShapeDtypeStruct((ms, n_cols), dt))
            is_side.append(True)
    assert in_tiles * tn == w.shape[1]
    outs = pl.pallas_call(
        functools.partial(_mm_segments_kernel, segs=tuple(segs)),
        grid=(m // tm, in_tiles),
        in_specs=[pl.BlockSpec((tm, k), lambda i, j: (i, 0)),
                  pl.BlockSpec((ms, k), lambda i, j: (0, 0)),
                  pl.BlockSpec((k, tn), lambda i, j: (0, j))],
        out_specs=out_specs, out_shape=out_shape,
        compiler_params=_params("arbitrary", "arbitrary"),
        name="w_in_segments",
    )(x, x_side, w)
    return ([o for o, s in zip(outs, is_side) if not s], [o for o, s in zip(outs, is_side) if s])


def _mm_ws_kernel(x_ref, xs_ref, w_ref, o_ref, os_ref, w_sc):
    first = pl.program_id(1) == 0
    _cast_weights_once([(w_ref, w_sc)], first)
    o_ref[...] = jnp.dot(x_ref[...], w_sc[...], preferred_element_type=F32).astype(o_ref.dtype)

    @pl.when(first)
    def _():
        os_ref[...] = jnp.dot(xs_ref[...], w_sc[...], preferred_element_type=F32).astype(os_ref.dtype)


def matmul_ws(x, x_side, w, out_dtype, *, tm=1024, tn=512, name):
    m, k = x.shape
    ms = x_side.shape[0]
    n = w.shape[1]
    tm, tn = min(tm, m), min(tn, n)
    return pl.pallas_call(
        _mm_ws_kernel,
        grid=(n // tn, m // tm),
        in_specs=[pl.BlockSpec((tm, k), lambda j, i: (i, 0)),
                  pl.BlockSpec((ms, k), lambda j, i: (0, 0)),
                  pl.BlockSpec((k, tn), lambda j, i: (0, j))],
        out_specs=[pl.BlockSpec((tm, tn), lambda j, i: (i, j)), pl.BlockSpec((ms, tn), lambda j, i: (0, j))],
        out_shape=[jax.ShapeDtypeStruct((m, n), out_dtype), jax.ShapeDtypeStruct((ms, n), out_dtype)],
        scratch_shapes=[pltpu.VMEM((k, tn), BF16)],
        compiler_params=_params("arbitrary", "arbitrary"),
        name=name,
    )(x, x_side, w)


def _cast_pad_rows_kernel(w_ref, o_ref, *, n_rows):
    tr = w_ref.shape[0]
    row = lax.broadcasted_iota(jnp.int32, w_ref.shape, 0) + pl.program_id(0) * tr
    o_ref[...] = jnp.where(row < n_rows, w_ref[...], 0.0).astype(o_ref.dtype)


def cast_pad_rows(w, n_rows_out, *, tr=512):
    r, c = w.shape
    return pl.pallas_call(
        functools.partial(_cast_pad_rows_kernel, n_rows=r),
        grid=(n_rows_out // tr,),
        in_specs=[pl.BlockSpec((tr, c), lambda i: (jnp.minimum(i, pl.cdiv(r, tr) - 1), 0))],
        out_specs=pl.BlockSpec((tr, c), lambda i: (i, 0)),
        out_shape=jax.ShapeDtypeStruct((n_rows_out, c), BF16),
        compiler_params=_params("parallel"),
        name="cast_pad_rows",
    )(w)


def _mm_kgrid_kernel(x_ref, xs_ref, w_ref, o_ref, os_ref, acc_sc, accs_sc):
    kk = pl.program_id(2)

    def accumulate(src_ref, acc_ref, dst_ref):
        part = jnp.dot(src_ref[...], w_ref[...], preferred_element_type=F32)

        @pl.when(kk == 0)
        def _():
            acc_ref[...] = part

        @pl.when(jnp.logical_and(kk != 0, kk != pl.num_programs(2) - 1))
        def _():
            acc_ref[...] += part

        @pl.when(kk == pl.num_programs(2) - 1)
        def _():
            dst_ref[...] = (acc_ref[...] + part).astype(dst_ref.dtype)

    accumulate(x_ref, acc_sc, o_ref)
    pl.when(pl.program_id(0) == 0)(functools.partial(accumulate, xs_ref, accs_sc, os_ref))


def matmul_kgrid(x, x_side, w, out_dtype, *, tm=1024, tn=1024, tk):
    m, k = x.shape
    ms = x_side.shape[0]
    n = w.shape[1]
    tm, tn = min(tm, m), min(tn, n)
    nj, nk = n // tn, k // tk
    assert nk >= 2
    return pl.pallas_call(
        _mm_kgrid_kernel,
        grid=(m // tm, nj, nk),
        in_specs=[pl.BlockSpec((tm, tk), lambda i, j, kk: (i, kk)),
                  pl.BlockSpec((ms, tk), lambda i, j, kk: (0, jnp.where(i == 0, kk, nk - 1))),
                  pl.BlockSpec((tk, tn), lambda i, j, kk: (kk, j))],
        out_specs=[pl.BlockSpec((tm, tn), lambda i, j, kk: (i, j)),
                   pl.BlockSpec((ms, tn), lambda i, j, kk: (0, jnp.where(i == 0, j, nj - 1)))],
        out_shape=[jax.ShapeDtypeStruct((m, n), out_dtype), jax.ShapeDtypeStruct((ms, n), out_dtype)],
        scratch_shapes=[pltpu.VMEM((tm, tn), F32), pltpu.VMEM((ms, tn), F32)],
        compiler_params=_params("arbitrary", "arbitrary", "arbitrary"),
        name="ffn_down",
    )(x, x_side, w)


def _cast_weights_once(pairs, do_cast):
    @pl.when(do_cast)
    def _():
        for src, dst in pairs:
            dst[...] = src[...].astype(BF16)


def _mm_swiglu_kernel(x_ref, xs_ref, wg_ref, wu_ref, o_ref, os_ref, wg_sc, wu_sc, *, n_real):
    j, i = pl.program_id(0), pl.program_id(1)
    real = j < n_real
    _cast_weights_once([(wg_ref, wg_sc), (wu_ref, wu_sc)], jnp.logical_and(i == 0, real))

    def emit(src_ref, dst_ref):
        x = src_ref[...]
        g = jnp.dot(x, wg_sc[...], preferred_element_type=F32)
        u = jnp.dot(x, wu_sc[...], preferred_element_type=F32)
        dst_ref[...] = (g * _sigmoid(g) * u).astype(dst_ref.dtype)

    def emit_zeros(dst_ref):
        dst_ref[...] = jnp.zeros(dst_ref.shape, dst_ref.dtype)

    pl.when(real)(functools.partial(emit, x_ref, o_ref))
    pl.when(jnp.logical_not(real))(functools.partial(emit_zeros, o_ref))
    pl.when(jnp.logical_and(i == 0, real))(functools.partial(emit, xs_ref, os_ref))
    pl.when(jnp.logical_and(i == 0, jnp.logical_not(real)))(functools.partial(emit_zeros, os_ref))


def matmul_swiglu(x, x_side, wg, wu, n_out, *, tm=1024, tn=256):
    m, k = x.shape
    ms = x_side.shape[0]
    n = wg.shape[1]
    tm = min(tm, m)
    n_real = n // tn
    w_spec = pl.BlockSpec((k, tn), lambda j, i: (0, jnp.minimum(j, n_real - 1)))
    return pl.pallas_call(
        functools.partial(_mm_swiglu_kernel, n_real=n_real),
        grid=(n_out // tn, m // tm),
        in_specs=[pl.BlockSpec((tm, k), lambda j, i: (jnp.where(j < n_real, i, 0), 0)),
                  pl.BlockSpec((ms, k), lambda j, i: (0, 0)), w_spec, w_spec],
        out_specs=[pl.BlockSpec((tm, tn), lambda j, i: (i, j)), pl.BlockSpec((ms, tn), lambda j, i: (0, j))],
        out_shape=[jax.ShapeDtypeStruct((m, n_out), BF16), jax.ShapeDtypeStruct((ms, n_out), BF16)],
        scratch_shapes=[pltpu.VMEM((k, tn), BF16), pltpu.VMEM((k, tn), BF16)],
        compiler_params=_params("arbitrary", "arbitrary"),
        name="ffn_up_swiglu",
    )(x, x_side, wg, wu)


def _mm_glu_kernel(x_ref, w_ref, e_ref, o_ref):
    acc = jnp.dot(x_ref[...].astype(BF16), w_ref[...], preferred_element_type=F32)
    o_ref[...] = (e_ref[...] * _sigmoid(acc)).astype(o_ref.dtype)


def matmul_glu(y, w, *, tm=1024, tn=512):
    m, k = y.shape
    tm, tn = min(tm, m), min(tn, k)
    return pl.pallas_call(
        _mm_glu_kernel,
        grid=(m // tm, k // tn),
        in_specs=[pl.BlockSpec((tm, k), lambda i, j: (i, 0)),
                  pl.BlockSpec((k, tn), lambda i, j: (0, j)),
                  pl.BlockSpec((tm, tn), lambda i, j: (i, j))],
        out_specs=pl.BlockSpec((tm, tn), lambda i, j: (i, j)),
        out_shape=jax.ShapeDtypeStruct((m, k), BF16),
        compiler_params=_params("parallel", "arbitrary"),
        name="ssm_glu",
    )(y, w, y)


def _mm_merge_kernel(xa_ref, ga_ref, xs_ref, gs_ref, xa2_ref, ga2_ref, xs2_ref, gs2_ref, wa_ref, ws_ref,
                     o_ref, o2_ref, wa_sc, ws_sc):
    first = pl.program_id(1) == 0
    _cast_weights_once([(wa_ref, wa_sc), (ws_ref, ws_sc)], first)

    def emit(xa, ga, xs, gs, dst_ref):
        ya = jnp.dot(xa[...], wa_sc[...], preferred_element_type=F32)
        ys = jnp.dot(xs[...], ws_sc[...], preferred_element_type=F32)
        merged = _sigmoid(ga[...].astype(F32)) * ya + _sigmoid(gs[...].astype(F32)) * ys
        dst_ref[...] = merged.astype(dst_ref.dtype)

    emit(xa_ref, ga_ref, xs_ref, gs_ref, o_ref)
    pl.when(first)(functools.partial(emit, xa2_ref, ga2_ref, xs2_ref, gs2_ref, o2_ref))


def matmul_merge(main, side, wa, ws, *, tm=1024, tn=512):
    m, ka = main[0].shape
    ms = side[0].shape[0]
    ks = main[2].shape[1]
    n = wa.shape[1]
    tm, tn = min(tm, m), min(tn, n)
    tile = pl.BlockSpec((tm, tn), lambda j, i: (i, j))
    tile2 = pl.BlockSpec((ms, tn), lambda j, i: (0, j))
    rows = lambda kdim: pl.BlockSpec((tm, kdim), lambda j, i: (i, 0))
    rows2 = lambda kdim: pl.BlockSpec((ms, kdim), lambda j, i: (0, 0))
    return pl.pallas_call(
        _mm_merge_kernel,
        grid=(n // tn, m // tm),
        in_specs=[rows(ka), tile, rows(ks), tile, rows2(ka), tile2, rows2(ks), tile2,
                  pl.BlockSpec((ka, tn), lambda j, i: (0, j)), pl.BlockSpec((ks, tn), lambda j, i: (0, j))],
        out_specs=[tile, tile2],
        out_shape=[jax.ShapeDtypeStruct((m, n), BF16), jax.ShapeDtypeStruct((ms, n), BF16)],
        scratch_shapes=[pltpu.VMEM((ka, tn), BF16), pltpu.VMEM((ks, tn), BF16)],
        compiler_params=_params("arbitrary", "arbitrary"),
        name="branch_merge",
    )(*main, *side, wa, ws)


def _mm_mod_kernel(c_ref, w_ref, b_ref, o_ref):
    c = c_ref[...]
    lhs = (c * _sigmoid(c)).astype(BF16)
    acc = jnp.dot(lhs, w_ref[...].astype(BF16), preferred_element_type=F32)
    o_ref[...] = acc + b_ref[...]


def matmul_mod(c, w, b, *, tn=512):
    m, k = c.shape
    n = w.shape[1]
    return pl.pallas_call(
        _mm_mod_kernel,
        grid=(n // tn,),
        in_specs=[pl.BlockSpec((m, k), lambda j: (0, 0)),
                  pl.BlockSpec((k, tn), lambda j: (0, j)),
                  pl.BlockSpec((1, tn), lambda j: (0, j))],
        out_specs=pl.BlockSpec((m, tn), lambda j: (0, j)),
        out_shape=jax.ShapeDtypeStruct((m, n), F32),
        compiler_params=_params("arbitrary"),
        name="adaln_mod",
    )(c, w, b)


def _modulate_kernel(x_ref, g_ref, sh_ref, sc_ref, h_ref):
    h = _rms(x_ref[0], g_ref[...]) * (1.0 + sc_ref[0]) + sh_ref[0]
    h_ref[0] = h.astype(h_ref.dtype)


def _resid_kernel(x_ref, y_ref, gpost_ref, gate_ref, *rest, coef, with_next):
    xn = x_ref[0] + coef * gate_ref[0] * _rms(y_ref[0].astype(F32), gpost_ref[...])
    if with_next:
        gpre_ref, sh_ref, sc_ref, xo_ref, h_ref = rest
        h = _rms(xn, gpre_ref[...]) * (1.0 + sc_ref[0]) + sh_ref[0]
        h_ref[0] = h.astype(h_ref.dtype)
    else:
        (xo_ref,) = rest
    xo_ref[0] = xn


def _row_specs(x, mod_rows, tr):
    g, t, d = x.shape
    tr = min(tr, t)
    row = pl.BlockSpec((1, tr, d), lambda gi, ti: (gi, ti, 0))
    gain = pl.BlockSpec((1, d), lambda gi, ti: (0, 0))
    if mod_rows == 1:
        mod = pl.BlockSpec((1, 1, d), lambda gi, ti: (gi, 0, 0))
    else:
        mod = pl.BlockSpec((1, tr, d), lambda gi, ti: (gi, ti, 0))
    return (g, t // tr), row, gain, mod


def modulate(x, gain, shift, scale, *, tr=256):
    grid, row, gspec, mod = _row_specs(x, shift.shape[1], tr)
    return pl.pallas_call(
        _modulate_kernel, grid=grid,
        in_specs=[row, gspec, mod, mod], out_specs=row,
        out_shape=jax.ShapeDtypeStruct(x.shape, BF16),
        compiler_params=_params("parallel", "parallel"),
        name="modulate",
    )(x, gain, shift, scale)


def resid_update(x, y, gpost, gate, coef, nxt=None, *, tr=256):
    grid, row, gspec, mod = _row_specs(x, gate.shape[1], tr)
    in_specs = [row, row, gspec, mod]
    args = [x, y, gpost, gate]
    out_specs = [row]
    out_shape = [jax.ShapeDtypeStruct(x.shape, F32)]
    if nxt is not None:
        in_specs += [gspec, mod, mod]
        args += list(nxt)
        out_specs.append(row)
        out_shape.append(jax.ShapeDtypeStruct(x.shape, BF16))
    outs = pl.pallas_call(
        functools.partial(_resid_kernel, coef=coef, with_next=nxt is not None), grid=grid,
        in_specs=in_specs, out_specs=out_specs, out_shape=out_shape,
        compiler_params=_params("parallel", "parallel"),
        name="resid_update",
    )(*args)
    return outs if nxt is not None else (outs[0], None)


def _diff_lambda(lamq_ref, lamk_ref, lam_init):
    prod = lamq_ref[...] * lamk_ref[...]
    s0 = jnp.sum(prod[0:1], axis=-1, keepdims=True)
    s1 = jnp.sum(prod[1:2], axis=-1, keepdims=True)
    return jnp.exp(s0) - jnp.exp(s1) + lam_init


def _decode_pages(qm_bf, k_refs, v_refs, state):
    m_prev, l_prev, acc_prev = state
    page, heads, _ = k_refs[0].shape[1:]
    n_rows, n_keys = 2 * heads, page * heads
    row = lax.broadcasted_iota(jnp.int32, (n_rows, n_keys), 0)
    col = lax.broadcasted_iota(jnp.int32, (n_rows, n_keys), 1)
    same_head = (col & (heads - 1)) == (row & (heads - 1))
    scores = []
    for k_ref in k_refs:
        k2 = k_ref[0].reshape(n_keys, HEAD_COLS).astype(BF16)
        s = lax.dot_general(qm_bf, k2, (((1,), (1,)), ((), ())), preferred_element_type=F32)
        scores.append(jnp.where(same_head, s, NEG))
    m_new = m_prev
    for s in scores:
        m_new = jnp.maximum(m_new, jnp.max(s, axis=-1, keepdims=True))
    corr = jnp.exp(m_prev - m_new)
    l_new = l_prev * corr
    acc = acc_prev * corr
    for s, v_ref in zip(scores, v_refs):
        p = jnp.exp(s - m_new)
        l_new = l_new + jnp.sum(p, axis=-1, keepdims=True)
        v2 = v_ref[0].reshape(n_keys, HEAD_COLS).astype(BF16)
        acc = acc + jnp.dot(p.astype(BF16), v2, preferred_element_type=F32)
    return m_new, l_new, acc


def _attn_kernel(qi_tab, ki_tab, pt_ref, lamq_ref, lamk_ref, subln_ref, q_ref, k_ref, v_ref,
                 qm_ref, kn_ref, vn_ref, *rest, n_pp, lam_init, steps_per_seq, n_dec_steps):
    k_refs, v_refs = rest[:n_pp], rest[n_pp:2 * n_pp]
    o_ref, od_ref, m_sc, l_sc, acc_sc, s_sc, p_sc, corr_sc, dm_sc, dl_sc, dacc_sc = rest[2 * n_pp:]
    t = pl.program_id(2)
    qi, ki = qi_tab[t], ki_tab[t]
    tq, tk = q_ref.shape[0], k_ref.shape[0]
    heads = kn_ref.shape[1]
    step = (pl.program_id(0) * pl.num_programs(1) + pl.program_id(1)) * pl.num_programs(2) + t
    page_group = lax.rem(step, steps_per_seq)
    qm = qm_ref[0]

    @pl.when(ki == 0)
    def _():
        m_sc[...] = jnp.full(m_sc.shape, NEG, F32)
        l_sc[...] = jnp.zeros(l_sc.shape, F32)
        acc_sc[...] = jnp.zeros(acc_sc.shape, F32)

    def advance(diagonal, with_decode):
        q, k, v = q_ref[...], k_ref[...], v_ref[...]
        for mp in range(2):
            cols = slice(mp * QK_DIM, (mp + 1) * QK_DIM)
            s_sc[mp] = lax.dot_general(q[:, cols], k[:, cols], (((1,), (1,)), ((), ())),
                                       preferred_element_type=F32)
        for mp in range(2):
            for r0 in range(0, tq, SOFTMAX_ROWS):
                rows = slice(r0, r0 + SOFTMAX_ROWS)
                s = s_sc[mp, rows, :]
                if diagonal:
                    row = lax.broadcasted_iota(jnp.int32, s.shape, 0) + r0
                    col = lax.broadcasted_iota(jnp.int32, s.shape, 1)
                    s = jnp.where(col <= row, s, NEG)
                m_prev = m_sc[mp, rows]
                m_new = jnp.maximum(m_prev, jnp.max(s, axis=-1, keepdims=True))
                corr = jnp.exp2(m_prev - m_new)
                p = jnp.exp2(s - m_new)
                l_sc[mp, rows] = l_sc[mp, rows] * corr + jnp.sum(p, axis=-1, keepdims=True)
                m_sc[mp, rows] = m_new
                corr_sc[mp, rows] = corr
                p_sc[mp, rows, :] = p.astype(BF16)
        for mp in range(2):
            acc_sc[mp] = acc_sc[mp] * corr_sc[mp] + jnp.dot(p_sc[mp], v, preferred_element_type=F32)
        if with_decode:
            fresh = page_group == 0
            state = (jnp.where(fresh, NEG, dm_sc[...]), jnp.where(fresh, 0.0, dl_sc[...]),
                     jnp.where(fresh, 0.0, dacc_sc[...]))
            dm_sc[...], dl_sc[...], dacc_sc[...] = _decode_pages(qm.astype(BF16), k_refs, v_refs, state)

    decoding = step < n_dec_steps
    for diagonal, on_block in ((False, ki < qi), (True, ki == qi)):
        for with_decode, on_step in ((True, decoding), (False, jnp.logical_not(decoding))):
            pl.when(jnp.logical_and(on_block, on_step))(functools.partial(advance, diagonal, with_decode))

    @pl.when(ki == qi)
    def _():
        lam = _diff_lambda(lamq_ref, lamk_ref, lam_init)
        o = acc_sc[0] / l_sc[0] - lam * (acc_sc[1] / l_sc[1])
        o_ref[...] = (_rms(o, subln_ref[...]) * (1.0 - lam_init)).astype(o_ref.dtype)

    @pl.when(jnp.logical_and(decoding, page_group == steps_per_seq - 1))
    def _():
        k_new = jnp.concatenate([kn_ref[0], kn_ref[0]], axis=0)
        v_new = jnp.concatenate([vn_ref[0], vn_ref[0]], axis=0)
        s_new = jnp.sum(qm * k_new, axis=-1, keepdims=True)
        m_prev = dm_sc[...]
        m_new = jnp.maximum(m_prev, s_new)
        corr = jnp.exp(m_prev - m_new)
        p_new = jnp.exp(s_new - m_new)
        w = (dacc_sc[...] * corr + p_new * v_new) / (dl_sc[...] * corr + p_new)
        lam = _diff_lambda(lamq_ref, lamk_ref, lam_init)
        o = w[:heads] - lam * w[heads:]
        od_ref[0] = (_rms(o, subln_ref[...]) * (1.0 - lam_init)).astype(od_ref.dtype)


def attention(q, k, v, qd, kd_new, vd_new, cache_k, cache_v, page_table, lam_q, lam_k, subln, bsz, lam_init,
              *, tq=512, min_pages_per_step=4):
    m, width = q.shape
    s_len = m // bsz
    heads = width // HEAD_COLS
    assert heads == SUBLANES, "cache pages are viewed as (page*heads, 256) row tiles"
    tq = min(tq, s_len)
    nq = s_len // tq
    pairs = [(a, b) for a in range(nq) for b in range(a + 1)]
    qi_tab = jnp.asarray(np.array([p[0] for p in pairs], np.int32))
    ki_tab = jnp.asarray(np.array([p[1] for p in pairs], np.int32))
    n_steps = bsz * heads * len(pairs)

    bd, n_pages = page_table.shape
    page = cache_k.shape[1]
    n_pp = next(c for c in range(min_pages_per_step, n_pages + 1)
                if n_pages % c == 0 and bd * (n_pages // c) <= n_steps)
    steps_per_seq = n_pages // n_pp
    n_dec_steps = bd * steps_per_seq
    q4 = qd.reshape(bd, heads, 2, QK_DIM).transpose(0, 2, 1, 3)
    qm = jnp.einsum('bmhd,mn->bmhnd', q4, jnp.eye(2, dtype=F32)).reshape(bd, 2 * heads, HEAD_COLS)

    def dec_step(b, h, t):
        return jnp.minimum((b * heads + h) * len(pairs) + t, n_dec_steps - 1)

    small = lambda shape: pl.BlockSpec(shape, lambda b, h, t, qt, kt, pt: (0, 0))
    tok = lambda rows: pl.BlockSpec(
        (1, rows, HEAD_COLS), lambda b, h, t, qt, kt, pt: (dec_step(b, h, t) // steps_per_seq, 0, 0))

    def page_spec(j):
        def index(b, h, t, qt, kt, pt):
            n = dec_step(b, h, t)
            return (pt[n // steps_per_seq, lax.rem(n, steps_per_seq) * n_pp + j], 0, 0, 0)
        return pl.BlockSpec((1, page, heads, HEAD_COLS), index)

    q_blk = pl.BlockSpec((tq, HEAD_COLS), lambda b, h, t, qt, kt, pt: (b * nq + qt[t], h))
    kv_blk = pl.BlockSpec((tq, HEAD_COLS), lambda b, h, t, qt, kt, pt: (b * nq + kt[t], h))
    grid_spec = pltpu.PrefetchScalarGridSpec(
        num_scalar_prefetch=3,
        grid=(bsz, heads, len(pairs)),
        in_specs=[small(lam_q.shape), small(lam_k.shape), small(subln.shape), q_blk, kv_blk, kv_blk,
                  tok(2 * heads), tok(heads), tok(heads)] + [page_spec(j) for j in range(n_pp)] * 2,
        out_specs=[q_blk, tok(heads)],
        scratch_shapes=[pltpu.VMEM((2, tq, 1), F32), pltpu.VMEM((2, tq, 1), F32),
                        pltpu.VMEM((2, tq, HEAD_COLS), F32),
                        pltpu.VMEM((2, tq, tq), F32), pltpu.VMEM((2, tq, tq), BF16), pltpu.VMEM((2, tq, 1), F32),
                        pltpu.VMEM((2 * heads, 1), F32), pltpu.VMEM((2 * heads, 1), F32),
                        pltpu.VMEM((2 * heads, HEAD_COLS), F32)])
    o, od = pl.pallas_call(
        functools.partial(_attn_kernel, n_pp=n_pp, lam_init=lam_init, steps_per_seq=steps_per_seq,
                          n_dec_steps=n_dec_steps),
        grid_spec=grid_spec,
        out_shape=[jax.ShapeDtypeStruct((m, width), BF16), jax.ShapeDtypeStruct((bd, heads, HEAD_COLS), BF16)],
        compiler_params=_params("arbitrary", "arbitrary", "arbitrary"),
        name="attention",
    )(qi_tab, ki_tab, page_table, lam_q, lam_k, subln, q, k, v, qm,
      kd_new.reshape(bd, heads, HEAD_COLS), vd_new.reshape(bd, heads, HEAD_COLS),
      *([cache_k] * n_pp), *([cache_v] * n_pp))
    return o, od.reshape(bd, width)


def _ssm_input_kernel(lr_ref, li_ref, ldt_ref, br_ref, bi_ref, bbr_ref, bbi_ref):
    lr, li = lr_ref[...], li_ref[...]
    dt = jnp.exp(ldt_ref[...])
    mag = jnp.exp(dt * lr)
    a_re, a_im = mag * jnp.cos(dt * li), mag * jnp.sin(dt * li)
    den = lr * lr + li * li
    f_re = ((a_re - 1.0) * lr + a_im * li) / den
    f_im = (a_im * lr - (a_re - 1.0) * li) / den
    br, bi = br_ref[...], bi_ref[...]
    bbr_ref[...] = f_re * br - f_im * bi
    bbi_ref[...] = f_re * bi + f_im * br


def _ssm_power_kernel(lr_ref, li_ref, ldt_ref, pr_ref, pi_ref):
    steps = (lax.broadcasted_iota(jnp.int32, pr_ref.shape, 0) + 1).astype(F32)
    dt = jnp.exp(ldt_ref[...])
    mag = jnp.exp(steps * (dt * lr_ref[...]))
    ang = steps * (dt * li_ref[...])
    pr_ref[...] = mag * jnp.cos(ang)
    pi_ref[...] = mag * jnp.sin(ang)


def ssm_prepare(lam_re, lam_im, log_dt, b_re, b_im, c_re, c_im):
    groups, nst = lam_re.shape
    gps = SSM_SUPER // SSM_GROUP
    n_super = groups // gps
    rep = lambda a: jnp.repeat(a, SSM_GROUP, axis=0)
    ldt_gn = jnp.broadcast_to(log_dt[:, None], (groups, nst))
    full = lambda a: pl.BlockSpec(a.shape, lambda: (0,) * a.ndim)
    bt = lambda b: b.transpose(0, 2, 1).reshape(groups * SSM_GROUP, nst)
    ins = [rep(lam_re), rep(lam_im), rep(ldt_gn), bt(b_re), bt(b_im)]
    bbr, bbi = pl.pallas_call(
        _ssm_input_kernel,
        in_specs=[full(a) for a in ins], out_specs=[full(ins[0])] * 2,
        out_shape=[jax.ShapeDtypeStruct(ins[0].shape, F32)] * 2,
    )(*ins)
    flat = lambda a: a.reshape(1, groups * nst)
    ins = [flat(lam_re), flat(lam_im), flat(ldt_gn)]
    pw = jax.ShapeDtypeStruct((SSM_SEG, groups * nst), F32)
    p_re, p_im = pl.pallas_call(
        _ssm_power_kernel,
        in_specs=[full(a) for a in ins], out_specs=[pl.BlockSpec(pw.shape, lambda: (0, 0))] * 2,
        out_shape=[pw, pw],
    )(*ins)
    eye = jnp.eye(gps, dtype=F32)

    def in_blocks(bb):
        x = bb.reshape(n_super, gps, SSM_GROUP, nst)
        return jnp.einsum('kgsn,gh->kgshn', x, eye).reshape(n_super, SSM_SUPER, gps * nst)

    def out_blocks(c):
        x = c.reshape(n_super, gps, SSM_GROUP, nst)
        return jnp.einsum('kgsn,gh->kgnhs', x, eye).reshape(n_super, gps * nst, SSM_SUPER).astype(BF16)

    wb = jnp.concatenate([in_blocks(bbr), in_blocks(bbi)], axis=-1).astype(BF16)
    return dict(wb=wb, wc_re=out_blocks(c_re), wc_im=out_blocks(c_im), p_re=p_re, p_im=p_im)


def _gelu_tanh(x):
    return 0.5 * x * (1.0 + jnp.tanh(math.sqrt(2.0 / math.pi) * (x + 0.044715 * (x * x * x))))


def _ssm_prompt_kernel(u_ref, wb_ref, wcr_ref, wci_ref, pr_ref, pi_ref, d_ref, y_ref, sre_ref, sim_ref,
                       perm_sc, bu_sc, xb_sc, car_sc):
    c = pl.program_id(2)
    seg = pr_ref.shape[0]
    nst = pr_ref.shape[1]
    re, im = slice(0, nst), slice(nst, 2 * nst)

    @pl.when(c == 0)
    def _():
        car_sc[...] = jnp.zeros(car_sc.shape, F32)

    n_lane_tiles = perm_sc.shape[0]
    for s in range(SUBLANES):
        for lt in range(n_lane_tiles):
            perm_sc[lt, pl.ds(s, seg, stride=SUBLANES), :] = (
                u_ref[0, s * seg:(s + 1) * seg, lt * LANES:(lt + 1) * LANES])
    up = jnp.concatenate([perm_sc[lt] for lt in range(n_lane_tiles)], axis=-1)
    bu_sc[...] = jnp.dot(up.astype(BF16), wb_ref[0], preferred_element_type=F32)

    a_re = jnp.broadcast_to(pr_ref[0:1, :], (SUBLANES, nst))
    a_im = jnp.broadcast_to(pi_ref[0:1, :], (SUBLANES, nst))

    def scan_body(i, carry):
        xr, xi = carry
        rows = pl.ds(pl.multiple_of(i * SUBLANES, SUBLANES), SUBLANES)
        nr = a_re * xr - a_im * xi + bu_sc[rows, re]
        ni = a_re * xi + a_im * xr + bu_sc[rows, im]
        bu_sc[rows, re] = nr
        bu_sc[rows, im] = ni
        return nr, ni

    zero = jnp.zeros((SUBLANES, nst), F32)
    er, ei = lax.fori_loop(0, seg, scan_body, (zero, zero), unroll=4)

    s_re, s_im = pr_ref[seg - 1:seg, :], pi_ref[seg - 1:seg, :]
    hr, hi = car_sc[0:1, :], car_sc[1:2, :]
    starts_r, starts_i = [], []
    for s in range(SUBLANES):
        starts_r.append(hr)
        starts_i.append(hi)
        hr, hi = (s_re * hr - s_im * hi + er[s:s + 1], s_re * hi + s_im * hr + ei[s:s + 1])
    car_sc[0:1, :] = hr
    car_sc[1:2, :] = hi
    h0r = jnp.concatenate(starts_r, axis=0)
    h0i = jnp.concatenate(starts_i, axis=0)

    def fix_body(i2, _):
        halves_r, halves_i = [], []
        for half in range(2):
            i = i2 * 2 + half
            rows = pl.ds(pl.multiple_of(i * SUBLANES, SUBLANES), SUBLANES)
            pr, pi = pr_ref[pl.ds(i, 1), :], pi_ref[pl.ds(i, 1), :]
            halves_r.append(bu_sc[rows, re] + (pr * h0r - pi * h0i))
            halves_i.append(bu_sc[rows, im] + (pr * h0i + pi * h0r))
        rows16 = pl.ds(pl.multiple_of(i2 * 2 * SUBLANES, 2 * SUBLANES), 2 * SUBLANES)
        xb_sc[rows16, re] = jnp.concatenate(halves_r, axis=0).astype(BF16)
        xb_sc[rows16, im] = jnp.concatenate(halves_i, axis=0).astype(BF16)
        return 0

    lax.fori_loop(0, seg // 2, fix_body, 0, unroll=2)

    y = (jnp.dot(xb_sc[:, re], wcr_ref[0], preferred_element_type=F32)
         - jnp.dot(xb_sc[:, im], wci_ref[0], preferred_element_type=F32)
         + d_ref[...] * up)
    y = _gelu_tanh(y)
    for lt in range(n_lane_tiles):
        perm_sc[lt] = y[:, lt * LANES:(lt + 1) * LANES]
    for s in range(SUBLANES):
        for lt in range(n_lane_tiles):
            y_ref[0, s * seg:(s + 1) * seg, lt * LANES:(lt + 1) * LANES] = (
                perm_sc[lt, pl.ds(s, seg, stride=SUBLANES), :])

    @pl.when(c == pl.num_programs(2) - 1)
    def _():
        sre_ref[0] = hr
        sim_ref[0] = hi


def ssm_prompt(u, prm, d):
    bsz, s_len, width = u.shape
    n_super = width // SSM_SUPER
    nst = prm['p_re'].shape[1] // n_super
    chunk = SUBLANES * SSM_SEG
    blk = pl.BlockSpec((1, chunk, SSM_SUPER), lambda b, k, c: (b, c, k))
    st = pl.BlockSpec((1, 1, nst), lambda b, k, c: (b, 0, k))
    pw = pl.BlockSpec((SSM_SEG, nst), lambda b, k, c: (0, k))
    y, sre, sim = pl.pallas_call(
        _ssm_prompt_kernel,
        grid=(bsz, n_super, s_len // chunk),
        in_specs=[blk,
                  pl.BlockSpec((1, SSM_SUPER, 2 * nst), lambda b, k, c: (k, 0, 0)),
                  pl.BlockSpec((1, nst, SSM_SUPER), lambda b, k, c: (k, 0, 0)),
                  pl.BlockSpec((1, nst, SSM_SUPER), lambda b, k, c: (k, 0, 0)),
                  pw, pw,
                  pl.BlockSpec((1, SSM_SUPER), lambda b, k, c: (0, k))],
        out_specs=[blk, st, st],
        out_shape=[jax.ShapeDtypeStruct(u.shape, F32),
                   jax.ShapeDtypeStruct((bsz, 1, n_super * nst), F32),
                   jax.ShapeDtypeStruct((bsz, 1, n_super * nst), F32)],
        scratch_shapes=[pltpu.VMEM((SSM_SUPER // LANES, chunk, LANES), F32), pltpu.VMEM((chunk, 2 * nst), F32),
                        pltpu.VMEM((chunk, 2 * nst), BF16), pltpu.VMEM((2, nst), F32)],
        compiler_params=_params("parallel", "parallel", "arbitrary"),
        name="ssm_prompt",
    )(u, prm['wb'], prm['wc_re'], prm['wc_im'], prm['p_re'], prm['p_im'], d)
    return y, sre[:, 0], sim[:, 0]


def _ssm_step_kernel(u_ref, x0r_ref, x0i_ref, wb_ref, wcr_ref, wci_ref, pr_ref, pi_ref, d_ref,
                     y_ref, sre_ref, sim_ref):
    nst = x0r_ref.shape[1]
    u = u_ref[...]
    bu = jnp.dot(u.astype(BF16), wb_ref[0], preferred_element_type=F32)
    a_re, a_im = pr_ref[0:1, :], pi_ref[0:1, :]
    x0r, x0i = x0r_ref[...], x0i_ref[...]
    xr = a_re * x0r - a_im * x0i + bu[:, :nst]
    xi = a_re * x0i + a_im * x0r + bu[:, nst:]
    y = (jnp.dot(xr.astype(BF16), wcr_ref[0], preferred_element_type=F32)
         - jnp.dot(xi.astype(BF16), wci_ref[0], preferred_element_type=F32)
         + d_ref[...] * u)
    y_ref[...] = _gelu_tanh(y)
    sre_ref[...] = xr
    sim_ref[...] = xi


def ssm_step(u, x0_re, x0_im, prm, d):
    bd, width = u.shape
    n_super = width // SSM_SUPER
    nst = x0_re.shape[1] // n_super
    ub = pl.BlockSpec((bd, SSM_SUPER), lambda k: (0, k))
    st = pl.BlockSpec((bd, nst), lambda k: (0, k))
    pw = pl.BlockSpec((SUBLANES, nst), lambda k: (0, k))
    return pl.pallas_call(
        _ssm_step_kernel,
        grid=(n_super,),
        in_specs=[ub, st, st,
                  pl.BlockSpec((1, SSM_SUPER, 2 * nst), lambda k: (k, 0, 0)),
                  pl.BlockSpec((1, nst, SSM_SUPER), lambda k: (k, 0, 0)),
                  pl.BlockSpec((1, nst, SSM_SUPER), lambda k: (k, 0, 0)),
                  pw, pw,
                  pl.BlockSpec((1, SSM_SUPER), lambda k: (0, k))],
        out_specs=[ub, st, st],
        out_shape=[jax.ShapeDtypeStruct(u.shape, F32), jax.ShapeDtypeStruct(x0_re.shape, F32),
                   jax.ShapeDtypeStruct(x0_im.shape, F32)],
        compiler_params=_params("parallel"),
        name="ssm_step",
    )(u, x0_re, x0_im, prm['wb'], prm['wc_re'], prm['wc_im'], prm['p_re'], prm['p_im'], d)


def _run_layer(xs, mods, wts, q_formats, attend, ssm_fns):
    groups = range(len(xs))
    d = xs[0].shape[-1]
    md = lambda i, sub, kind: mods[i][:, :, sub * 3 + kind, :]
    flat = lambda a: a.reshape(-1, a.shape[-1])
    gpre, gpost = wts['norm_pre'], wts['norm_post']
    att_w = wts['w_branch_attn'].shape[0]
    ssm_w = wts['w_branch_ssm'].shape[0]

    def ffn(hs, gate_w, up_w, down_w):
        acts = matmul_swiglu(flat(hs[0]), flat(hs[1]), gate_w, up_w, down_w.shape[0])
        ys = matmul_kgrid(*acts, down_w, BF16, tk=down_w.shape[0] // 4)
        return [ys[i].reshape(hs[i].shape) for i in groups]

    def sublayer_end(xs, ys, sub, coef, with_next):
        nxt = lambda i: (gpre[sub + 1:sub + 2], md(i, sub + 1, 0), md(i, sub + 1, 1)) if with_next else None
        res = [resid_update(xs[i], ys[i], gpost[sub:sub + 1], md(i, sub, 2), coef, nxt(i)) for i in groups]
        return [r[0] for r in res], [r[1] for r in res]

    hs = [modulate(xs[i], gpre[0:1], md(i, 0, 0), md(i, 0, 1)) for i in groups]
    ys = ffn(hs, wts['ffn1_gate'], wts['ffn1_up'], wts['ffn1_down'])
    xs, hs = sublayer_end(xs, ys, 0, 0.5, True)

    both = lambda dtypes: ((1.0, dtypes), (1.0, dtypes))
    proj = matmul_segments(flat(hs[0]), flat(hs[1]), wts['w_in'], [
        (att_w, *[(scale, [dt]) for scale, dt in q_formats]), (att_w, *both([F32, BF16])),
        (att_w, *both([F32, BF16])), (ssm_w, *both([F32])), (d, *both([BF16])), (d, *both([BF16]))])
    os = attend(proj)
    branches, states = [], []
    for i in groups:
        q, k32, k16, v32, v16, u, g_att, g_ssm = proj[i]
        y_s, s_re, s_im = ssm_fns[i](u)
        branches.append((os[i], g_att, matmul_glu(flat(y_s), wts['ssm_w_glu']), g_ssm))
        states.append((k32, v32, s_re, s_im))
    merged = matmul_merge(*branches, wts['w_branch_attn'], wts['w_branch_ssm'])
    ys = matmul_ws(*merged, wts['w_out'], BF16, name="w_out")
    xs, hs = sublayer_end(xs, [ys[i].reshape(xs[i].shape) for i in groups], 1, 1.0, True)

    ys = ffn(hs, wts['ffn2_gate'], wts['ffn2_up'], wts['ffn2_down'])
    xs, _ = sublayer_end(xs, ys, 2, 0.5, False)
    return xs, states


def kernel(x_prompt, x_sample, cache_k, cache_v, state_ssm_re, state_ssm_im, page_table, c_prompt, c_sample, w_mod, b_mod, norm_pre, norm_post, ffn1_gate, ffn1_up, ffn1_down, w_in, lam_q, lam_k, attn_subln, w_branch_attn, ssm_lam_re, ssm_lam_im, ssm_log_dt, ssm_b_re, ssm_b_im, ssm_c_re, ssm_c_im, ssm_d, ssm_w_glu, w_branch_ssm, w_out, ffn2_gate, ffn2_up, ffn2_down):
    depth = w_mod.shape[0]
    bsz, s_len, d = x_prompt.shape
    bd, dec_seq, _ = x_sample.shape
    assert dec_seq == 1, "the decode attention handles one new token per sequence"
    heads = cache_k.shape[3]
    d_ff = ffn1_gate.shape[2]
    d_ff_pad = _round_up(d_ff, 1024)
    groups, nst = ssm_lam_re.shape[1:]

    xp, xs = x_prompt, x_sample.reshape(1, bd, d)
    outs = [[] for _ in range(8)]
    for li in range(depth):
        lam_init = 0.8 - 0.6 * math.exp(-0.3 * li)
        wts = dict(norm_pre=norm_pre[li], norm_post=norm_post[li],
                   ffn1_gate=ffn1_gate[li], ffn1_up=ffn1_up[li], ffn1_down=cast_pad_rows(ffn1_down[li], d_ff_pad),
                   ffn2_gate=ffn2_gate[li], ffn2_up=ffn2_up[li], ffn2_down=cast_pad_rows(ffn2_down[li], d_ff_pad),
                   w_in=w_in[li].astype(BF16), w_branch_attn=w_branch_attn[li], w_branch_ssm=w_branch_ssm[li],
                   ssm_w_glu=ssm_w_glu[li].astype(BF16), w_out=w_out[li])
        ssm_prm = ssm_prepare(ssm_lam_re[li], ssm_lam_im[li], ssm_log_dt[li], ssm_b_re[li], ssm_b_im[li],
                              ssm_c_re[li], ssm_c_im[li])
        d_row = ssm_d[li].reshape(1, -1)
        subln = attn_subln[li].reshape(1, -1)

        n_cond = bsz + bd
        c_all = jnp.pad(jnp.concatenate([c_prompt, c_sample], axis=0), ((0, _round_up(n_cond, 16) - n_cond), (0, 0)))
        mod = matmul_mod(c_all, w_mod[li], b_mod[li].reshape(1, -1))
        mod_p = mod[:bsz].reshape(bsz, 1, N_SUB * 3, d)
        mod_s = mod[bsz:n_cond].reshape(1, bd, N_SUB * 3, d)

        def attend(proj):
            (q, _, k16, _, v16, *_), (qd, kd32, _, vd32, *_) = proj
            return attention(q, k16, v16, qd, kd32, vd32, cache_k[li], cache_v[li], page_table,
                             lam_q[li], lam_k[li], subln, bsz, lam_init)

        def ssm_p(u):
            return ssm_prompt(u.reshape(bsz, s_len, -1), ssm_prm, d_row)

        def ssm_s(u):
            return ssm_step(u, state_ssm_re[li].reshape(bd, -1), state_ssm_im[li].reshape(bd, -1), ssm_prm, d_row)

        q_formats = [(QK_DIM ** -0.5 * LOG2_E, BF16), (QK_DIM ** -0.5, F32)]
        (xp, xs), ((k1, v1, r1, i1), (k2, v2, r2, i2)) = _run_layer(
            [xp, xs], [mod_p, mod_s], wts, q_formats, attend, [ssm_p, ssm_s])
        new = [k1.reshape(bsz, s_len, heads, HEAD_COLS), v1.reshape(bsz, s_len, heads, HEAD_COLS),
               r1.reshape(bsz, groups, nst), i1.reshape(bsz, groups, nst),
               k2.reshape(bd, 1, heads, HEAD_COLS), v2.reshape(bd, 1, heads, HEAD_COLS),
               r2.reshape(bd, groups, nst), i2.reshape(bd, groups, nst)]
        for acc, val in zip(outs, new):
            acc.append(val)
    return (xp, xs.reshape(bd, 1, d), *[jnp.stack(o) for o in outs])
```

```python
import functools
import math

import numpy as np
import jax
import jax.numpy as jnp
from jax import lax
from jax.experimental import pallas as pl
from jax.experimental.pallas import tpu as pltpu

F32 = jnp.float32
BF16 = jnp.bfloat16
EPS = 1e-6
NEG = -1e30

LOG2_E = math.log2(math.e)
QK_DIM = 128
HEAD_COLS = 2 * QK_DIM
SSM_GROUP = 16
SSM_STATE = 64
N_SUB = 3
SUBLANES = 8
LANES = 128
SSM_SUPER = 256
SSM_SEG = 64
SOFTMAX_ROWS = 64
VMEM_LIMIT_BYTES = 56 * 1024 * 1024


def _params(*sem):
    return pltpu.CompilerParams(dimension_semantics=sem, vmem_limit_bytes=VMEM_LIMIT_BYTES)


def _round_up(x, m):
    return (x + m - 1) // m * m


def _sigmoid(x):
    return 0.5 * jnp.tanh(0.5 * x) + 0.5


def _rms(x, gain):
    return x * lax.rsqrt(jnp.mean(x * x, axis=-1, keepdims=True) + EPS) * gain


def _mm_segments_kernel(x_ref, xs_ref, w_ref, *o_refs, segs):
    i, j = pl.program_id(0), pl.program_id(1)
    pos = 0
    for start, count, (scale_m, dtypes_m), (scale_s, dtypes_s) in segs:
        outs_m = o_refs[pos:pos + len(dtypes_m)]
        outs_s = o_refs[pos + len(dtypes_m):pos + len(dtypes_m) + len(dtypes_s)]
        pos += len(dtypes_m) + len(dtypes_s)
        in_seg = jnp.logical_and(j >= start, j < start + count)

        def emit(src_ref, outs, scale):
            acc = jnp.dot(src_ref[...], w_ref[...], preferred_element_type=F32)
            if scale != 1.0:
                acc = acc * scale
            for o_ref in outs:
                o_ref[...] = acc.astype(o_ref.dtype)

        pl.when(in_seg)(functools.partial(emit, x_ref, outs_m, scale_m))
        pl.when(jnp.logical_and(in_seg, i == 0))(functools.partial(emit, xs_ref, outs_s, scale_s))


def matmul_segments(x, x_side, w, segments, *, tm=1024, tn=512):
    m, k = x.shape
    ms = x_side.shape[0]
    tm = min(tm, m)
    segs, in_tiles, out_specs, out_shape, is_side = [], 0, [], [], []
    for n_cols, fmt_m, fmt_s in segments:
        start, count = in_tiles, n_cols // tn
        segs.append((start, count, (fmt_m[0], tuple(fmt_m[1])), (fmt_s[0], tuple(fmt_s[1]))))
        in_tiles += count
        for dt in fmt_m[1]:
            out_specs.append(pl.BlockSpec(
                (tm, tn), lambda i, j, start=start, count=count: (i, jnp.clip(j - start, 0, count - 1))))
            out_shape.append(jax.ShapeDtypeStruct((m, n_cols), dt))
            is_side.append(False)
        for dt in fmt_s[1]:
            out_specs.append(pl.BlockSpec(
                (ms, tn), lambda i, j, start=start, count=count:
                (0, jnp.where(i == 0, jnp.clip(j - start, 0, count - 1), count - 1))))
            out_shape.append(jax.ShapeDtypeStruct((ms, n_cols), dt))
            is_side.append(True)
    assert in_tiles * tn == w.shape[1]
    outs = pl.pallas_call(
        functools.partial(_mm_segments_kernel, segs=tuple(segs)),
        grid=(m // tm, in_tiles),
        in_specs=[pl.BlockSpec((tm, k), lambda i, j: (i, 0)),
                  pl.BlockSpec((ms, k), lambda i, j: (0, 0)),
                  pl.BlockSpec((k, tn), lambda i, j: (0, j))],
        out_specs=out_specs, out_shape=out_shape,
        compiler_params=_params("arbitrary", "arbitrary"),
        name="w_in_segments",
    )(x, x_side, w)
    return ([o for o, s in zip(outs, is_side) if not s], [o for o, s in zip(outs, is_side) if s])


def _mm_ws_kernel(x_ref, xs_ref, w_ref, o_ref, os_ref, w_sc):
    first = pl.program_id(1) == 0
    _cast_weights_once([(w_ref, w_sc)], first)
    o_ref[...] = jnp.dot(x_ref[...], w_sc[...], preferred_element_type=F32).astype(o_ref.dtype)

    @pl.when(first)
    def _():
        os_ref[...] = jnp.dot(xs_ref[...], w_sc[...], preferred_element_type=F32).astype(os_ref.dtype)


def matmul_ws(x, x_side, w, out_dtype, *, tm=1024, tn=512, name):
    m, k = x.shape
    ms = x_side.shape[0]
    n = w.shape[1]
    tm, tn = min(tm, m), min(tn, n)
    return pl.pallas_call(
        _mm_ws_kernel,
        grid=(n // tn, m // tm),
        in_specs=[pl.BlockSpec((tm, k), lambda j, i: (i, 0)),
                  pl.BlockSpec((ms, k), lambda j, i: (0, 0)),
                  pl.BlockSpec((k, tn), lambda j, i: (0, j))],
        out_specs=[pl.BlockSpec((tm, tn), lambda j, i: (i, j)), pl.BlockSpec((ms, tn), lambda j, i: (0, j))],
        out_shape=[jax.ShapeDtypeStruct((m, n), out_dtype), jax.ShapeDtypeStruct((ms, n), out_dtype)],
        scratch_shapes=[pltpu.VMEM((k, tn), BF16)],
        compiler_params=_params("arbitrary", "arbitrary"),
        name=name,
    )(x, x_side, w)


def _cast_pad_rows_kernel(w_ref, o_ref, *, n_rows):
    tr = w_ref.shape[0]
    row = lax.broadcasted_iota(jnp.int32, w_ref.shape, 0) + pl.program_id(0) * tr
    o_ref[...] = jnp.where(row < n_rows, w_ref[...], 0.0).astype(o_ref.dtype)


def cast_pad_rows(w, n_rows_out, *, tr=512):
    r, c = w.shape
    return pl.pallas_call(
        functools.partial(_cast_pad_rows_kernel, n_rows=r),
        grid=(n_rows_out // tr,),
        in_specs=[pl.BlockSpec((tr, c), lambda i: (jnp.minimum(i, pl.cdiv(r, tr) - 1), 0))],
        out_specs=pl.BlockSpec((tr, c), lambda i: (i, 0)),
        out_shape=jax.ShapeDtypeStruct((n_rows_out, c), BF16),
        compiler_params=_params("parallel"),
        name="cast_pad_rows",
    )(w)


def _mm_kgrid_kernel(x_ref, xs_ref, w_ref, o_ref, os_ref, acc_sc, accs_sc):
    kk = pl.program_id(2)

    def accumulate(src_ref, acc_ref, dst_ref):
        part = jnp.dot(src_ref[...], w_ref[...], preferred_element_type=F32)

        @pl.when(kk == 0)
        def _():
            acc_ref[...] = part

        @pl.when(jnp.logical_and(kk != 0, kk != pl.num_programs(2) - 1))
        def _():
            acc_ref[...] += part

        @pl.when(kk == pl.num_programs(2) - 1)
        def _():
            dst_ref[...] = (acc_ref[...] + part).astype(dst_ref.dtype)

    accumulate(x_ref, acc_sc, o_ref)
    pl.when(pl.program_id(0) == 0)(functools.partial(accumulate, xs_ref, accs_sc, os_ref))


def matmul_kgrid(x, x_side, w, out_dtype, *, tm=1024, tn=1024, tk):
    m, k = x.shape
    ms = x_side.shape[0]
    n = w.shape[1]
    tm, tn = min(tm, m), min(tn, n)
    nj, nk = n // tn, k // tk
    assert nk >= 2
    return pl.pallas_call(
        _mm_kgrid_kernel,
        grid=(m // tm, nj, nk),
        in_specs=[pl.BlockSpec((tm, tk), lambda i, j, kk: (i, kk)),
                  pl.BlockSpec((ms, tk), lambda i, j, kk: (0, jnp.where(i == 0, kk, nk - 1))),
                  pl.BlockSpec((tk, tn), lambda i, j, kk: (kk, j))],
        out_specs=[pl.BlockSpec((tm, tn), lambda i, j, kk: (i, j)),
                   pl.BlockSpec((ms, tn), lambda i, j, kk: (0, jnp.where(i == 0, j, nj - 1)))],
        out_shape=[jax.ShapeDtypeStruct((m, n), out_dtype), jax.ShapeDtypeStruct((ms, n), out_dtype)],
        scratch_shapes=[pltpu.VMEM((tm, tn), F32), pltpu.VMEM((ms, tn), F32)],
        compiler_params=_params("arbitrary", "arbitrary", "arbitrary"),
        name="ffn_down",
    )(x, x_side, w)


def _cast_weights_once(pairs, do_cast):
    @pl.when(do_cast)
    def _():
        for src, dst in pairs:
            dst[...] = src[...].astype(BF16)


def _mm_swiglu_kernel(x_ref, xs_ref, wg_ref, wu_ref, o_ref, os_ref, wg_sc, wu_sc, *, n_real):
    j, i = pl.program_id(0), pl.program_id(1)
    real = j < n_real
    _cast_weights_once([(wg_ref, wg_sc), (wu_ref, wu_sc)], jnp.logical_and(i == 0, real))

    def emit(src_ref, dst_ref):
        x = src_ref[...]
        g = jnp.dot(x, wg_sc[...], preferred_element_type=F32)
        u = jnp.dot(x, wu_sc[...], preferred_element_type=F32)
        dst_ref[...] = (g * _sigmoid(g) * u).astype(dst_ref.dtype)

    def emit_zeros(dst_ref):
        dst_ref[...] = jnp.zeros(dst_ref.shape, dst_ref.dtype)

    pl.when(real)(functools.partial(emit, x_ref, o_ref))
    pl.when(jnp.logical_not(real))(functools.partial(emit_zeros, o_ref))
    pl.when(jnp.logical_and(i == 0, real))(functools.partial(emit, xs_ref, os_ref))
    pl.when(jnp.logical_and(i == 0, jnp.logical_not(real)))(functools.partial(emit_zeros, os_ref))


def matmul_swiglu(x, x_side, wg, wu, n_out, *, tm=1024, tn=256):
    m, k = x.shape
    ms = x_side.shape[0]
    n = wg.shape[1]
    tm = min(tm, m)
    n_real = n // tn
    w_spec = pl.BlockSpec((k, tn), lambda j, i: (0, jnp.minimum(j, n_real - 1)))
    return pl.pallas_call(
        functools.partial(_mm_swiglu_kernel, n_real=n_real),
        grid=(n_out // tn, m // tm),
        in_specs=[pl.BlockSpec((tm, k), lambda j, i: (jnp.where(j < n_real, i, 0), 0)),
                  pl.BlockSpec((ms, k), lambda j, i: (0, 0)), w_spec, w_spec],
        out_specs=[pl.BlockSpec((tm, tn), lambda j, i: (i, j)), pl.BlockSpec((ms, tn), lambda j, i: (0, j))],
        out_shape=[jax.ShapeDtypeStruct((m, n_out), BF16), jax.ShapeDtypeStruct((ms, n_out), BF16)],
        scratch_shapes=[pltpu.VMEM((k, tn), BF16), pltpu.VMEM((k, tn), BF16)],
        compiler_params=_params("arbitrary", "arbitrary"),
        name="ffn_up_swiglu",
    )(x, x_side, wg, wu)


def _mm_glu_kernel(x_ref, w_ref, e_ref, o_ref):
    acc = jnp.dot(x_ref[...].astype(BF16), w_ref[...], preferred_element_type=F32)
    o_ref[...] = (e_ref[...] * _sigmoid(acc)).astype(o_ref.dtype)


def matmul_glu(y, w, *, tm=1024, tn=512):
    m, k = y.shape
    tm, tn = min(tm, m), min(tn, k)
    return pl.pallas_call(
        _mm_glu_kernel,
        grid=(m // tm, k // tn),
        in_specs=[pl.BlockSpec((tm, k), lambda i, j: (i, 0)),
                  pl.BlockSpec((k, tn), lambda i, j: (0, j)),
                  pl.BlockSpec((tm, tn), lambda i, j: (i, j))],
        out_specs=pl.BlockSpec((tm, tn), lambda i, j: (i, j)),
        out_shape=jax.ShapeDtypeStruct((m, k), BF16),
        compiler_params=_params("parallel", "arbitrary"),
        name="ssm_glu",
    )(y, w, y)


def _mm_merge_kernel(xa_ref, ga_ref, xs_ref, gs_ref, xa2_ref, ga2_ref, xs2_ref, gs2_ref, wa_ref, ws_ref,
                     o_ref, o2_ref, wa_sc, ws_sc):
    first = pl.program_id(1) == 0
    _cast_weights_once([(wa_ref, wa_sc), (ws_ref, ws_sc)], first)

    def emit(xa, ga, xs, gs, dst_ref):
        ya = jnp.dot(xa[...], wa_sc[...], preferred_element_type=F32)
        ys = jnp.dot(xs[...], ws_sc[...], preferred_element_type=F32)
        merged = _sigmoid(ga[...].astype(F32)) * ya + _sigmoid(gs[...].astype(F32)) * ys
        dst_ref[...] = merged.astype(dst_ref.dtype)

    emit(xa_ref, ga_ref, xs_ref, gs_ref, o_ref)
    pl.when(first)(functools.partial(emit, xa2_ref, ga2_ref, xs2_ref, gs2_ref, o2_ref))


def matmul_merge(main, side, wa, ws, *, tm=1024, tn=512):
    m, ka = main[0].shape
    ms = side[0].shape[0]
    ks = main[2].shape[1]
    n = wa.shape[1]
    tm, tn = min(tm, m), min(tn, n)
    tile = pl.BlockSpec((tm, tn), lambda j, i: (i, j))
    tile2 = pl.BlockSpec((ms, tn), lambda j, i: (0, j))
    rows = lambda kdim: pl.BlockSpec((tm, kdim), lambda j, i: (i, 0))
    rows2 = lambda kdim: pl.BlockSpec((ms, kdim), lambda j, i: (0, 0))
    return pl.pallas_call(
        _mm_merge_kernel,
        grid=(n // tn, m // tm),
        in_specs=[rows(ka), tile, rows(ks), tile, rows2(ka), tile2, rows2(ks), tile2,
                  pl.BlockSpec((ka, tn), lambda j, i: (0, j)), pl.BlockSpec((ks, tn), lambda j, i: (0, j))],
        out_specs=[tile, tile2],
        out_shape=[jax.ShapeDtypeStruct((m, n), BF16), jax.ShapeDtypeStruct((ms, n), BF16)],
        scratch_shapes=[pltpu.VMEM((ka, tn), BF16), pltpu.VMEM((ks, tn), BF16)],
        compiler_params=_params("arbitrary", "arbitrary"),
        name="branch_merge",
    )(*main, *side, wa, ws)


def _mm_mod_kernel(c_ref, w_ref, b_ref, o_ref):
    c = c_ref[...]
    lhs = (c * _sigmoid(c)).astype(BF16)
    acc = jnp.dot(lhs, w_ref[...].astype(BF16), preferred_element_type=F32)
    o_ref[...] = acc + b_ref[...]


def matmul_mod(c, w, b, *, tn=512):
    m, k = c.shape
    n = w.shape[1]
    return pl.pallas_call(
        _mm_mod_kernel,
        grid=(n // tn,),
        in_specs=[pl.BlockSpec((m, k), lambda j: (0, 0)),
                  pl.BlockSpec((k, tn), lambda j: (0, j)),
                  pl.BlockSpec((1, tn), lambda j: (0, j))],
        out_specs=pl.BlockSpec((m, tn), lambda j: (0, j)),
        out_shape=jax.ShapeDtypeStruct((m, n), F32),
        compiler_params=_params("arbitrary"),
        name="adaln_mod",
    )(c, w, b)


def _modulate_kernel(x_ref, g_ref, sh_ref, sc_ref, h_ref):
    h = _rms(x_ref[0], g_ref[...]) * (1.0 + sc_ref[0]) + sh_ref[0]
    h_ref[0] = h.astype(h_ref.dtype)


def _resid_kernel(x_ref, y_ref, gpost_ref, gate_ref, *rest, coef, with_next):
    xn = x_ref[0] + coef * gate_ref[0] * _rms(y_ref[0].astype(F32), gpost_ref[...])
    if with_next:
        gpre_ref, sh_ref, sc_ref, xo_ref, h_ref = rest
        h = _rms(xn, gpre_ref[...]) * (1.0 + sc_ref[0]) + sh_ref[0]
        h_ref[0] = h.astype(h_ref.dtype)
    else:
        (xo_ref,) = rest
    xo_ref[0] = xn


def _row_specs(x, mod_rows, tr):
    g, t, d = x.shape
    tr = min(tr, t)
    row = pl.BlockSpec((1, tr, d), lambda gi, ti: (gi, ti, 0))
    gain = pl.BlockSpec((1, d), lambda gi, ti: (0, 0))
    if mod_rows == 1:
        mod = pl.BlockSpec((1, 1, d), lambda gi, ti: (gi, 0, 0))
    else:
        mod = pl.BlockSpec((1, tr, d), lambda gi, ti: (gi, ti, 0))
    return (g, t // tr), row, gain, mod


def modulate(x, gain, shift, scale, *, tr=256):
    grid, row, gspec, mod = _row_specs(x, shift.shape[1], tr)
    return pl.pallas_call(
        _modulate_kernel, grid=grid,
        in_specs=[row, gspec, mod, mod], out_specs=row,
        out_shape=jax.ShapeDtypeStruct(x.shape, BF16),
        compiler_params=_params("parallel", "parallel"),
        name="modulate",
    )(x, gain, shift, scale)


def resid_update(x, y, gpost, gate, coef, nxt=None, *, tr=256):
    grid, row, gspec, mod = _row_specs(x, gate.shape[1], tr)
    in_specs = [row, row, gspec, mod]
    args = [x, y, gpost, gate]
    out_specs = [row]
    out_shape = [jax.ShapeDtypeStruct(x.shape, F32)]
    if nxt is not None:
        in_specs += [gspec, mod, mod]
        args += list(nxt)
        out_specs.append(row)
        out_shape.append(jax.ShapeDtypeStruct(x.shape, BF16))
    outs = pl.pallas_call(
        functools.partial(_resid_kernel, coef=coef, with_next=nxt is not None), grid=grid,
        in_specs=in_specs, out_specs=out_specs, out_shape=out_shape,
        compiler_params=_params("parallel", "parallel"),
        name="resid_update",
    )(*args)
    return outs if nxt is not None else (outs[0], None)


def _diff_lambda(lamq_ref, lamk_ref, lam_init):
    prod = lamq_ref[...] * lamk_ref[...]
    s0 = jnp.sum(prod[0:1], axis=-1, keepdims=True)
    s1 = jnp.sum(prod[1:2], axis=-1, keepdims=True)
    return jnp.exp(s0) - jnp.exp(s1) + lam_init


def _decode_pages(qm_bf, k_refs, v_refs, state):
    m_prev, l_prev, acc_prev = state
    page, heads, _ = k_refs[0].shape[1:]
    n_rows, n_keys = 2 * heads, page * heads
    row = lax.broadcasted_iota(jnp.int32, (n_rows, n_keys), 0)
    col = lax.broadcasted_iota(jnp.int32, (n_rows, n_keys), 1)
    same_head = (col & (heads - 1)) == (row & (heads - 1))
    scores = []
    for k_ref in k_refs:
        k2 = k_ref[0].reshape(n_keys, HEAD_COLS).astype(BF16)
        s = lax.dot_general(qm_bf, k2, (((1,), (1,)), ((), ())), preferred_element_type=F32)
        scores.append(jnp.where(same_head, s, NEG))
    m_new = m_prev
    for s in scores:
        m_new = jnp.maximum(m_new, jnp.max(s, axis=-1, keepdims=True))
    corr = jnp.exp(m_prev - m_new)
    l_new = l_prev * corr
    acc = acc_prev * corr
    for s, v_ref in zip(scores, v_refs):
        p = jnp.exp(s - m_new)
        l_new = l_new + jnp.sum(p, axis=-1, keepdims=True)
        v2 = v_ref[0].reshape(n_keys, HEAD_COLS).astype(BF16)
        acc = acc + jnp.dot(p.astype(BF16), v2, preferred_element_type=F32)
    return m_new, l_new, acc


def _attn_kernel(qi_tab, ki_tab, pt_ref, lamq_ref, lamk_ref, subln_ref, q_ref, k_ref, v_ref,
                 qm_ref, kn_ref, vn_ref, *rest, n_pp, lam_init, steps_per_seq, n_dec_steps):
    k_refs, v_refs = rest[:n_pp], rest[n_pp:2 * n_pp]
    o_ref, od_ref, m_sc, l_sc, acc_sc, s_sc, p_sc, corr_sc, dm_sc, dl_sc, dacc_sc = rest[2 * n_pp:]
    t = pl.program_id(2)
    qi, ki = qi_tab[t], ki_tab[t]
    tq, tk = q_ref.shape[0], k_ref.shape[0]
    heads = kn_ref.shape[1]
    step = (pl.program_id(0) * pl.num_programs(1) + pl.program_id(1)) * pl.num_programs(2) + t
    page_group = lax.rem(step, steps_per_seq)
    qm = qm_ref[0]

    @pl.when(ki == 0)
    def _():
        m_sc[...] = jnp.full(m_sc.shape, NEG, F32)
        l_sc[...] = jnp.zeros(l_sc.shape, F32)
        acc_sc[...] = jnp.zeros(acc_sc.shape, F32)

    def advance(diagonal, with_decode):
        if with_decode:
            fresh = page_group == 0
            state = (jnp.where(fresh, NEG, dm_sc[...]), jnp.where(fresh, 0.0, dl_sc[...]),
                     jnp.where(fresh, 0.0, dacc_sc[...]))
            dm_sc[...], dl_sc[...], dacc_sc[...] = _decode_pages(qm.astype(BF16), k_refs, v_refs, state)
        q, k, v = q_ref[...], k_ref[...], v_ref[...]
        for mp in range(2):
            cols = slice(mp * QK_DIM, (mp + 1) * QK_DIM)
            s_sc[mp] = lax.dot_general(q[:, cols], k[:, cols], (((1,), (1,)), ((), ())),
                                       preferred_element_type=F32)
        for mp in range(2):
            for r0 in range(0, tq, SOFTMAX_ROWS):
                rows = slice(r0, r0 + SOFTMAX_ROWS)
                s = s_sc[mp, rows, :]
                if diagonal:
                    row = lax.broadcasted_iota(jnp.int32, s.shape, 0) + r0
                    col = lax.broadcasted_iota(jnp.int32, s.shape, 1)
                    s = jnp.where(col <= row, s, NEG)
                m_prev = m_sc[mp, rows]
                m_new = jnp.maximum(m_prev, jnp.max(s, axis=-1, keepdims=True))
                corr = jnp.exp2(m_prev - m_new)
                p = jnp.exp2(s - m_new)
                l_sc[mp, rows] = l_sc[mp, rows] * corr + jnp.sum(p, axis=-1, keepdims=True)
                m_sc[mp, rows] = m_new
                corr_sc[mp, rows] = corr
                p_sc[mp, rows, :] = p.astype(BF16)
        for mp in range(2):
            acc_sc[mp] = acc_sc[mp] * corr_sc[mp] + jnp.dot(p_sc[mp], v, preferred_element_type=F32)

    decoding = step < n_dec_steps
    for diagonal, on_block in ((False, ki < qi), (True, ki == qi)):
        for with_decode, on_step in ((True, decoding), (False, jnp.logical_not(decoding))):
            pl.when(jnp.logical_and(on_block, on_step))(functools.partial(advance, diagonal, with_decode))

    @pl.when(ki == qi)
    def _():
        lam = _diff_lambda(lamq_ref, lamk_ref, lam_init)
        o = acc_sc[0] / l_sc[0] - lam * (acc_sc[1] / l_sc[1])
        o_ref[...] = (_rms(o, subln_ref[...]) * (1.0 - lam_init)).astype(o_ref.dtype)

    @pl.when(jnp.logical_and(decoding, page_group == steps_per_seq - 1))
    def _():
        k_new = jnp.concatenate([kn_ref[0], kn_ref[0]], axis=0)
        v_new = jnp.concatenate([vn_ref[0], vn_ref[0]], axis=0)
        s_new = jnp.sum(qm * k_new, axis=-1, keepdims=True)
        m_prev = dm_sc[...]
        m_new = jnp.maximum(m_prev, s_new)
        corr = jnp.exp(m_prev - m_new)
        p_new = jnp.exp(s_new - m_new)
        w = (dacc_sc[...] * corr + p_new * v_new) / (dl_sc[...] * corr + p_new)
        lam = _diff_lambda(lamq_ref, lamk_ref, lam_init)
        o = w[:heads] - lam * w[heads:]
        od_ref[0] = (_rms(o, subln_ref[...]) * (1.0 - lam_init)).astype(od_ref.dtype)


def attention(q, k, v, qd, kd_new, vd_new, cache_k, cache_v, page_table, lam_q, lam_k, subln, bsz, lam_init,
              *, tq=512, min_pages_per_step=4):
    m, width = q.shape
    s_len = m // bsz
    heads = width // HEAD_COLS
    assert heads == SUBLANES, "cache pages are viewed as (page*heads, 256) row tiles"
    tq = min(tq, s_len)
    nq = s_len // tq
    pairs = [(a, b) for a in range(nq) for b in range(a + 1)]
    qi_tab = jnp.asarray(np.array([p[0] for p in pairs], np.int32))
    ki_tab = jnp.asarray(np.array([p[1] for p in pairs], np.int32))
    n_steps = bsz * heads * len(pairs)

    bd, n_pages = page_table.shape
    page = cache_k.shape[1]
    n_pp = next(c for c in range(min_pages_per_step, n_pages + 1)
                if n_pages % c == 0 and bd * (n_pages // c) <= n_steps)
    steps_per_seq = n_pages // n_pp
    n_dec_steps = bd * steps_per_seq
    q4 = qd.reshape(bd, heads, 2, QK_DIM).transpose(0, 2, 1, 3)
    qm = jnp.einsum('bmhd,mn->bmhnd', q4, jnp.eye(2, dtype=F32)).reshape(bd, 2 * heads, HEAD_COLS)

    def dec_step(b, h, t):
        return jnp.minimum((b * heads + h) * len(pairs) + t, n_dec_steps - 1)

    small = lambda shape: pl.BlockSpec(shape, lambda b, h, t, qt, kt, pt: (0, 0))
    tok = lambda rows: pl.BlockSpec(
        (1, rows, HEAD_COLS), lambda b, h, t, qt, kt, pt: (dec_step(b, h, t) // steps_per_seq, 0, 0))

    def page_spec(j):
        def index(b, h, t, qt, kt, pt):
            n = dec_step(b, h, t)
            return (pt[n // steps_per_seq, lax.rem(n, steps_per_seq) * n_pp + j], 0, 0, 0)
        return pl.BlockSpec((1, page, heads, HEAD_COLS), index)

    q_blk = pl.BlockSpec((tq, HEAD_COLS), lambda b, h, t, qt, kt, pt: (b * nq + qt[t], h))
    kv_blk = pl.BlockSpec((tq, HEAD_COLS), lambda b, h, t, qt, kt, pt: (b * nq + kt[t], h))
    grid_spec = pltpu.PrefetchScalarGridSpec(
        num_scalar_prefetch=3,
        grid=(bsz, heads, len(pairs)),
        in_specs=[small(lam_q.shape), small(lam_k.shape), small(subln.shape), q_blk, kv_blk, kv_blk,
                  tok(2 * heads), tok(heads), tok(heads)] + [page_spec(j) for j in range(n_pp)] * 2,
        out_specs=[q_blk, tok(heads)],
        scratch_shapes=[pltpu.VMEM((2, tq, 1), F32), pltpu.VMEM((2, tq, 1), F32),
                        pltpu.VMEM((2, tq, HEAD_COLS), F32),
                        pltpu.VMEM((2, tq, tq), F32), pltpu.VMEM((2, tq, tq), BF16), pltpu.VMEM((2, tq, 1), F32),
                        pltpu.VMEM((2 * heads, 1), F32), pltpu.VMEM((2 * heads, 1), F32),
                        pltpu.VMEM((2 * heads, HEAD_COLS), F32)])
    o, od = pl.pallas_call(
        functools.partial(_attn_kernel, n_pp=n_pp, lam_init=lam_init, steps_per_seq=steps_per_seq,
                          n_dec_steps=n_dec_steps),
        grid_spec=grid_spec,
        out_shape=[jax.ShapeDtypeStruct((m, width), BF16), jax.ShapeDtypeStruct((bd, heads, HEAD_COLS), BF16)],
        compiler_params=_params("arbitrary", "arbitrary", "arbitrary"),
        name="attention",
    )(qi_tab, ki_tab, page_table, lam_q, lam_k, subln, q, k, v, qm,
      kd_new.reshape(bd, heads, HEAD_COLS), vd_new.reshape(bd, heads, HEAD_COLS),
      *([cache_k] * n_pp), *([cache_v] * n_pp))
    return o, od.reshape(bd, width)


def _ssm_input_kernel(lr_ref, li_ref, ldt_ref, br_ref, bi_ref, bbr_ref, bbi_ref):
    lr, li = lr_ref[...], li_ref[...]
    dt = jnp.exp(ldt_ref[...])
    mag = jnp.exp(dt * lr)
    a_re, a_im = mag * jnp.cos(dt * li), mag * jnp.sin(dt * li)
    den = lr * lr + li * li
    f_re = ((a_re - 1.0) * lr + a_im * li) / den
    f_im = (a_im * lr - (a_re - 1.0) * li) / den
    br, bi = br_ref[...], bi_ref[...]
    bbr_ref[...] = f_re * br - f_im * bi
    bbi_ref[...] = f_re * bi + f_im * br


def _ssm_power_kernel(lr_ref, li_ref, ldt_ref, pr_ref, pi_ref):
    steps = (lax.broadcasted_iota(jnp.int32, pr_ref.shape, 0) + 1).astype(F32)
    dt = jnp.exp(ldt_ref[...])
    mag = jnp.exp(steps * (dt * lr_ref[...]))
    ang = steps * (dt * li_ref[...])
    pr_ref[...] = mag * jnp.cos(ang)
    pi_ref[...] = mag * jnp.sin(ang)


def ssm_prepare(lam_re, lam_im, log_dt, b_re, b_im, c_re, c_im):
    groups, nst = lam_re.shape
    gps = SSM_SUPER // SSM_GROUP
    n_super = groups // gps
    rep = lambda a: jnp.repeat(a, SSM_GROUP, axis=0)
    ldt_gn = jnp.broadcast_to(log_dt[:, None], (groups, nst))
    full = lambda a: pl.BlockSpec(a.shape, lambda: (0,) * a.ndim)
    bt = lambda b: b.transpose(0, 2, 1).reshape(groups * SSM_GROUP, nst)
    ins = [rep(lam_re), rep(lam_im), rep(ldt_gn), bt(b_re), bt(b_im)]
    bbr, bbi = pl.pallas_call(
        _ssm_input_kernel,
        in_specs=[full(a) for a in ins], out_specs=[full(ins[0])] * 2,
        out_shape=[jax.ShapeDtypeStruct(ins[0].shape, F32)] * 2,
    )(*ins)
    flat = lambda a: a.reshape(1, groups * nst)
    ins = [flat(lam_re), flat(lam_im), flat(ldt_gn)]
    pw = jax.ShapeDtypeStruct((SSM_SEG, groups * nst), F32)
    p_re, p_im = pl.pallas_call(
        _ssm_power_kernel,
        in_specs=[full(a) for a in ins], out_specs=[pl.BlockSpec(pw.shape, lambda: (0, 0))] * 2,
        out_shape=[pw, pw],
    )(*ins)
    eye = jnp.eye(gps, dtype=F32)

    def in_blocks(bb):
        x = bb.reshape(n_super, gps, SSM_GROUP, nst)
        return jnp.einsum('kgsn,gh->kgshn', x, eye).reshape(n_super, SSM_SUPER, gps * nst)

    def out_blocks(c):
        x = c.reshape(n_super, gps, SSM_GROUP, nst)
        return jnp.einsum('kgsn,gh->kgnhs', x, eye).reshape(n_super, gps * nst, SSM_SUPER).astype(BF16)

    wb = jnp.concatenate([in_blocks(bbr), in_blocks(bbi)], axis=-1).astype(BF16)
    return dict(wb=wb, wc_re=out_blocks(c_re), wc_im=out_blocks(c_im), p_re=p_re, p_im=p_im)


def _gelu_tanh(x):
    return 0.5 * x * (1.0 + jnp.tanh(math.sqrt(2.0 / math.pi) * (x + 0.044715 * (x * x * x))))


def _ssm_prompt_kernel(u_ref, wb_ref, wcr_ref, wci_ref, pr_ref, pi_ref, d_ref, y_ref, sre_ref, sim_ref,
                       perm_sc, bu_sc, xb_sc, car_sc):
    c = pl.program_id(2)
    seg = pr_ref.shape[0]
    nst = pr_ref.shape[1]
    re, im = slice(0, nst), slice(nst, 2 * nst)

    @pl.when(c == 0)
    def _():
        car_sc[...] = jnp.zeros(car_sc.shape, F32)

    n_lane_tiles = perm_sc.shape[0]
    for s in range(SUBLANES):
        for lt in range(n_lane_tiles):
            perm_sc[lt, pl.ds(s, seg, stride=SUBLANES), :] = (
                u_ref[0, s * seg:(s + 1) * seg, lt * LANES:(lt + 1) * LANES])
    up = jnp.concatenate([perm_sc[lt] for lt in range(n_lane_tiles)], axis=-1)
    bu_sc[...] = jnp.dot(up.astype(BF16), wb_ref[0], preferred_element_type=F32)

    a_re = jnp.broadcast_to(pr_ref[0:1, :], (SUBLANES, nst))
    a_im = jnp.broadcast_to(pi_ref[0:1, :], (SUBLANES, nst))

    def scan_body(i, carry):
        xr, xi = carry
        rows = pl.ds(pl.multiple_of(i * SUBLANES, SUBLANES), SUBLANES)
        nr = a_re * xr - a_im * xi + bu_sc[rows, re]
        ni = a_re * xi + a_im * xr + bu_sc[rows, im]
        bu_sc[rows, re] = nr
        bu_sc[rows, im] = ni
        return nr, ni

    zero = jnp.zeros((SUBLANES, nst), F32)
    er, ei = lax.fori_loop(0, seg, scan_body, (zero, zero), unroll=4)

    s_re, s_im = pr_ref[seg - 1:seg, :], pi_ref[seg - 1:seg, :]
    hr, hi = car_sc[0:1, :], car_sc[1:2, :]
    starts_r, starts_i = [], []
    for s in range(SUBLANES):
        starts_r.append(hr)
        starts_i.append(hi)
        hr, hi = (s_re * hr - s_im * hi + er[s:s + 1], s_re * hi + s_im * hr + ei[s:s + 1])
    car_sc[0:1, :] = hr
    car_sc[1:2, :] = hi
    h0r = jnp.concatenate(starts_r, axis=0)
    h0i = jnp.concatenate(starts_i, axis=0)

    def fix_body(i2, _):
        halves_r, halves_i = [], []
        for half in range(2):
            i = i2 * 2 + half
            rows = pl.ds(pl.multiple_of(i * SUBLANES, SUBLANES), SUBLANES)
            pr, pi = pr_ref[pl.ds(i, 1), :], pi_ref[pl.ds(i, 1), :]
            halves_r.append(bu_sc[rows, re] + (pr * h0r - pi * h0i))
            halves_i.append(bu_sc[rows, im] + (pr * h0i + pi * h0r))
        rows16 = pl.ds(pl.multiple_of(i2 * 2 * SUBLANES, 2 * SUBLANES), 2 * SUBLANES)
        xb_sc[rows16, re] = jnp.concatenate(halves_r, axis=0).astype(BF16)
        xb_sc[rows16, im] = jnp.concatenate(halves_i, axis=0).astype(BF16)
        return 0

    lax.fori_loop(0, seg // 2, fix_body, 0, unroll=2)

    y = (jnp.dot(xb_sc[:, re], wcr_ref[0], preferred_element_type=F32)
         - jnp.dot(xb_sc[:, im], wci_ref[0], preferred_element_type=F32)
         + d_ref[...] * up)
    y = _gelu_tanh(y)
    for lt in range(n_lane_tiles):
        perm_sc[lt] = y[:, lt * LANES:(lt + 1) * LANES]
    for s in range(SUBLANES):
        for lt in range(n_lane_tiles):
            y_ref[0, s * seg:(s + 1) * seg, lt * LANES:(lt + 1) * LANES] = (
                perm_sc[lt, pl.ds(s, seg, stride=SUBLANES), :])

    @pl.when(c == pl.num_programs(2) - 1)
    def _():
        sre_ref[0] = hr
        sim_ref[0] = hi


def ssm_prompt(u, prm, d):
    bsz, s_len, width = u.shape
    n_super = width // SSM_SUPER
    nst = prm['p_re'].shape[1] // n_super
    chunk = SUBLANES * SSM_SEG
    blk = pl.BlockSpec((1, chunk, SSM_SUPER), lambda b, k, c: (b, c, k))
    st = pl.BlockSpec((1, 1, nst), lambda b, k, c: (b, 0, k))
    pw = pl.BlockSpec((SSM_SEG, nst), lambda b, k, c: (0, k))
    y, sre, sim = pl.pallas_call(
        _ssm_prompt_kernel,
        grid=(bsz, n_super, s_len // chunk),
        in_specs=[blk,
                  pl.BlockSpec((1, SSM_SUPER, 2 * nst), lambda b, k, c: (k, 0, 0)),
                  pl.BlockSpec((1, nst, SSM_SUPER), lambda b, k, c: (k, 0, 0)),
                  pl.BlockSpec((1, nst, SSM_SUPER), lambda b, k, c: (k, 0, 0)),
                  pw, pw,
                  pl.BlockSpec((1, SSM_SUPER), lambda b, k, c: (0, k))],
        out_specs=[blk, st, st],
        out_shape=[jax.ShapeDtypeStruct(u.shape, F32),
                   jax.ShapeDtypeStruct((bsz, 1, n_super * nst), F32),
                   jax.ShapeDtypeStruct((bsz, 1, n_super * nst), F32)],
        scratch_shapes=[pltpu.VMEM((SSM_SUPER // LANES, chunk, LANES), F32), pltpu.VMEM((chunk, 2 * nst), F32),
                        pltpu.VMEM((chunk, 2 * nst), BF16), pltpu.VMEM((2, nst), F32)],
        compiler_params=_params("parallel", "parallel", "arbitrary"),
        name="ssm_prompt",
    )(u, prm['wb'], prm['wc_re'], prm['wc_im'], prm['p_re'], prm['p_im'], d)
    return y, sre[:, 0], sim[:, 0]


def _ssm_step_kernel(u_ref, x0r_ref, x0i_ref, wb_ref, wcr_ref, wci_ref, pr_ref, pi_ref, d_ref,
                     y_ref, sre_ref, sim_ref):
    nst = x0r_ref.shape[1]
    u = u_ref[...]
    bu = jnp.dot(u.astype(BF16), wb_ref[0], preferred_element_type=F32)
    a_re, a_im = pr_ref[0:1, :], pi_ref[0:1, :]
    x0r, x0i = x0r_ref[...], x0i_ref[...]
    xr = a_re * x0r - a_im * x0i + bu[:, :nst]
    xi = a_re * x0i + a_im * x0r + bu[:, nst:]
    y = (jnp.dot(xr.astype(BF16), wcr_ref[0], preferred_element_type=F32)
         - jnp.dot(xi.astype(BF16), wci_ref[0], preferred_element_type=F32)
         + d_ref[...] * u)
    y_ref[...] = _gelu_tanh(y)
    sre_ref[...] = xr
    sim_ref[...] = xi


def ssm_step(u, x0_re, x0_im, prm, d):
    bd, width = u.shape
    n_super = width // SSM_SUPER
    nst = x0_re.shape[1] // n_super
    ub = pl.BlockSpec((bd, SSM_SUPER), lambda k: (0, k))
    st = pl.BlockSpec((bd, nst), lambda k: (0, k))
    pw = pl.BlockSpec((SUBLANES, nst), lambda k: (0, k))
    return pl.pallas_call(
        _ssm_step_kernel,
        grid=(n_super,),
        in_specs=[ub, st, st,
                  pl.BlockSpec((1, SSM_SUPER, 2 * nst), lambda k: (k, 0, 0)),
                  pl.BlockSpec((1, nst, SSM_SUPER), lambda k: (k, 0, 0)),
                  pl.BlockSpec((1, nst, SSM_SUPER), lambda k: (k, 0, 0)),
                  pw, pw,
                  pl.BlockSpec((1, SSM_SUPER), lambda k: (0, k))],
        out_specs=[ub, st, st],
        out_shape=[jax.ShapeDtypeStruct(u.shape, F32), jax.ShapeDtypeStruct(x0_re.shape, F32),
                   jax.ShapeDtypeStruct(x0_im.shape, F32)],
        compiler_params=_params("parallel"),
        name="ssm_step",
    )(u, x0_re, x0_im, prm['wb'], prm['wc_re'], prm['wc_im'], prm['p_re'], prm['p_im'], d)


def _run_layer(xs, mods, wts, q_formats, attend, ssm_fns):
    groups = range(len(xs))
    d = xs[0].shape[-1]
    md = lambda i, sub, kind: mods[i][:, :, sub * 3 + kind, :]
    flat = lambda a: a.reshape(-1, a.shape[-1])
    gpre, gpost = wts['norm_pre'], wts['norm_post']
    att_w = wts['w_branch_attn'].shape[0]
    ssm_w = wts['w_branch_ssm'].shape[0]

    def ffn(hs, gate_w, up_w, down_w):
        acts = matmul_swiglu(flat(hs[0]), flat(hs[1]), gate_w, up_w, down_w.shape[0])
        ys = matmul_kgrid(*acts, down_w, BF16, tk=down_w.shape[0] // 4)
        return [ys[i].reshape(hs[i].shape) for i in groups]

    def sublayer_end(xs, ys, sub, coef, with_next):
        nxt = lambda i: (gpre[sub + 1:sub + 2], md(i, sub + 1, 0), md(i, sub + 1, 1)) if with_next else None
        res = [resid_update(xs[i], ys[i], gpost[sub:sub + 1], md(i, sub, 2), coef, nxt(i)) for i in groups]
        return [r[0] for r in res], [r[1] for r in res]

    hs = [modulate(xs[i], gpre[0:1], md(i, 0, 0), md(i, 0, 1)) for i in groups]
    ys = ffn(hs, wts['ffn1_gate'], wts['ffn1_up'], wts['ffn1_down'])
    xs, hs = sublayer_end(xs, ys, 0, 0.5, True)

    both = lambda dtypes: ((1.0, dtypes), (1.0, dtypes))
    proj = matmul_segments(flat(hs[0]), flat(hs[1]), wts['w_in'], [
        (att_w, *[(scale, [dt]) for scale, dt in q_formats]), (att_w, *both([F32, BF16])),
        (att_w, *both([F32, BF16])), (ssm_w, *both([F32])), (d, *both([BF16])), (d, *both([BF16]))])
    os = attend(proj)
    branches, states = [], []
    for i in groups:
        q, k32, k16, v32, v16, u, g_att, g_ssm = proj[i]
        y_s, s_re, s_im = ssm_fns[i](u)
        branches.append((os[i], g_att, matmul_glu(flat(y_s), wts['ssm_w_glu']), g_ssm))
        states.append((k32, v32, s_re, s_im))
    merged = matmul_merge(*branches, wts['w_branch_attn'], wts['w_branch_ssm'])
    ys = matmul_ws(*merged, wts['w_out'], BF16, name="w_out")
    xs, hs = sublayer_end(xs, [ys[i].reshape(xs[i].shape) for i in groups], 1, 1.0, True)

    ys = ffn(hs, wts['ffn2_gate'], wts['ffn2_up'], wts['ffn2_down'])
    xs, _ = sublayer_end(xs, ys, 2, 0.5, False)
    return xs, states


def kernel(x_prompt, x_sample, cache_k, cache_v, state_ssm_re, state_ssm_im, page_table, c_prompt, c_sample, w_mod, b_mod, norm_pre, norm_post, ffn1_gate, ffn1_up, ffn1_down, w_in, lam_q, lam_k, attn_subln, w_branch_attn, ssm_lam_re, ssm_lam_im, ssm_log_dt, ssm_b_re, ssm_b_im, ssm_c_re, ssm_c_im, ssm_d, ssm_w_glu, w_branch_ssm, w_out, ffn2_gate, ffn2_up, ffn2_down):
    depth = w_mod.shape[0]
    bsz, s_len, d = x_prompt.shape
    bd, dec_seq, _ = x_sample.shape
    assert dec_seq == 1, "the decode attention handles one new token per sequence"
    heads = cache_k.shape[3]
    d_ff = ffn1_gate.shape[2]
    d_ff_pad = _round_up(d_ff, 1024)
    groups, nst = ssm_lam_re.shape[1:]

    xp, xs = x_prompt, x_sample.reshape(1, bd, d)
    outs = [[] for _ in range(8)]
    for li in range(depth):
        lam_init = 0.8 - 0.6 * math.exp(-0.3 * li)
        wts = dict(norm_pre=norm_pre[li], norm_post=norm_post[li],
                   ffn1_gate=ffn1_gate[li], ffn1_up=ffn1_up[li], ffn1_down=cast_pad_rows(ffn1_down[li], d_ff_pad),
                   ffn2_gate=ffn2_gate[li], ffn2_up=ffn2_up[li], ffn2_down=cast_pad_rows(ffn2_down[li], d_ff_pad),
                   w_in=w_in[li].astype(BF16), w_branch_attn=w_branch_attn[li], w_branch_ssm=w_branch_ssm[li],
                   ssm_w_glu=ssm_w_glu[li].astype(BF16), w_out=w_out[li])
        ssm_prm = ssm_prepare(ssm_lam_re[li], ssm_lam_im[li], ssm_log_dt[li], ssm_b_re[li], ssm_b_im[li],
                              ssm_c_re[li], ssm_c_im[li])
        d_row = ssm_d[li].reshape(1, -1)
        subln = attn_subln[li].reshape(1, -1)

        n_cond = bsz + bd
        c_all = jnp.pad(jnp.concatenate([c_prompt, c_sample], axis=0), ((0, _round_up(n_cond, 16) - n_cond), (0, 0)))
        mod = matmul_mod(c_all, w_mod[li], b_mod[li].reshape(1, -1))
        mod_p = mod[:bsz].reshape(bsz, 1, N_SUB * 3, d)
        mod_s = mod[bsz:n_cond].reshape(1, bd, N_SUB * 3, d)

        def attend(proj):
            (q, _, k16, _, v16, *_), (qd, kd32, _, vd32, *_) = proj
            return attention(q, k16, v16, qd, kd32, vd32, cache_k[li], cache_v[li], page_table,
                             lam_q[li], lam_k[li], subln, bsz, lam_init)

        def ssm_p(u):
            return ssm_prompt(u.reshape(bsz, s_len, -1), ssm_prm, d_row)

        def ssm_s(u):
            return ssm_step(u, state_ssm_re[li].reshape(bd, -1), state_ssm_im[li].reshape(bd, -1), ssm_prm, d_row)

        q_formats = [(QK_DIM ** -0.5 * LOG2_E, BF16), (QK_DIM ** -0.5, F32)]
        (xp, xs), ((k1, v1, r1, i1), (k2, v2, r2, i2)) = _run_layer(
            [xp, xs], [mod_p, mod_s], wts, q_formats, attend, [ssm_p, ssm_s])
        new = [k1.reshape(bsz, s_len, heads, HEAD_COLS), v1.reshape(bsz, s_len, heads, HEAD_COLS),
               r1.reshape(bsz, groups, nst), i1.reshape(bsz, groups, nst),
               k2.reshape(bd, 1, heads, HEAD_COLS), v2.reshape(bd, 1, heads, HEAD_COLS),
               r2.reshape(bd, groups, nst), i2.reshape(bd, groups, nst)]
        for acc, val in zip(outs, new):
            acc.append(val)
    return (xp, xs.reshape(bd, 1, d), *[jnp.stack(o) for o in outs])
```
